```python
import math, functools
import jax, jax.numpy as jnp
from jax import lax
import numpy as np

D_MODEL = 1024
BATCH = 8
SEQ = 4096
DEPTH = 2
DEC_BATCH = 128
DEC_SEQ = 8
PAST_LEN = 16384
PAGE_SIZE = 128

N_A_LAYERS = DEPTH // 2
N_B_LAYERS = DEPTH - N_A_LAYERS
N_DENSE = (DEPTH + 1) // 2
N_MOE = DEPTH // 2
HGRN_EXPAND = 128
HGRN_HEADS = D_MODEL // HGRN_EXPAND
HGRN_DK = HGRN_EXPAND
HGRN_DV = D_MODEL // HGRN_HEADS
HGRN_CHUNK = 64
MLA_HEADS = 8
QK_NOPE = 128
QK_ROPE = 64
V_DIM = 128
KV_LORA = D_MODEL // 4
Q_LORA = (3 * D_MODEL) // 8
ROPE_THETA = 10000.0
Q_BLOCK = 128
SM_SCALE = (QK_NOPE + QK_ROPE) ** -0.5
NEG_INF = -1e30
D_FF = 2816
N_EXPERTS = 8
TOP_K = 2
D_FF_EXPERT = 2816
RMS_EPS = 1e-6

kernel_name = "hgrn2_mla_yoco_moe_step"


def _rmsnorm(x, g):
    xf = x.astype(jnp.float32)
    y = xf * lax.rsqrt(jnp.mean(xf * xf, axis=-1, keepdims=True) + RMS_EPS)
    return (y * g.astype(jnp.float32)).astype(x.dtype)


def _rope(x, pos):
    half = x.shape[-1] // 2
    inv = ROPE_THETA ** (-jnp.arange(half, dtype=jnp.float32) / half)
    ang = pos.astype(jnp.float32)[:, None] * inv[None, :]
    shape = (pos.shape[0],) + (1,) * (x.ndim - 3) + (half,)
    cos = jnp.cos(ang).reshape(shape)
    sin = jnp.sin(ang).reshape(shape)
    xf = x.astype(jnp.float32)
    x1, x2 = xf[..., :half], xf[..., half:]
    return jnp.concatenate([x1 * cos - x2 * sin, x2 * cos + x1 * sin], axis=-1).astype(x.dtype)


def _gla_recurrence(q, k, v, log_f, S0):
    B, T, H, DK = q.shape
    DV = v.shape[-1]
    L = HGRN_CHUNK if T % HGRN_CHUNK == 0 else T
    n = T // L

    def blocks(a):
        return a.reshape(B, n, L, H, a.shape[-1]).transpose(1, 0, 3, 2, 4)

    t = jnp.arange(L)
    causal = (t[:, None] >= t[None, :])[:, :, None]

    def step(S, inp):
        qc, kc, vc, gc = inp
        b = jnp.cumsum(gc, axis=2)
        o_inter = jnp.einsum('bhtk,bhkv->bhtv', qc * jnp.exp(b), S)
        diff = b[:, :, :, None, :] - b[:, :, None, :, :]
        decay = jnp.where(causal, jnp.exp(jnp.minimum(diff, 0.0)), 0.0)
        scores = jnp.einsum('bhtk,bhsk,bhtsk->bhts', qc, kc, decay)
        o_intra = jnp.einsum('bhts,bhsv->bhtv', scores, vc)
        b_last = b[:, :, -1:, :]
        S_new = (jnp.exp(b_last[:, :, 0, :])[..., None] * S
                 + jnp.einsum('bhsk,bhsv->bhkv', kc * jnp.exp(b_last - b), vc))
        return S_new, o_inter + o_intra

    S_T, o = lax.scan(step, S0, (blocks(q), blocks(k), blocks(v), blocks(log_f)))
    o = o.transpose(1, 0, 3, 2, 4).reshape(B, T, H, DV)
    return o, S_T


def _hgrn2_mixer(xn, S0, w_in, lb, g_o, w_out):
    B, T, _ = xn.shape
    hk = HGRN_HEADS * HGRN_DK
    hv = HGRN_HEADS * HGRN_DV
    proj = xn @ w_in
    q, z, i, g = jnp.split(proj, [hk, 2 * hk, 2 * hk + hv], axis=-1)
    zf = z.astype(jnp.float32)
    lb = lb.astype(jnp.float32)
    log_f = jnp.logaddexp(jnp.log(lb), jnp.log1p(-lb) + jax.nn.log_sigmoid(zf))
    k = (1.0 - lb) * jax.nn.sigmoid(-zf)
    qf = jax.nn.silu(q.astype(jnp.float32))
    shp_k = (B, T, HGRN_HEADS, HGRN_DK)
    shp_v = (B, T, HGRN_HEADS, HGRN_DV)
    o, S_T = _gla_recurrence(qf.reshape(shp_k), k.reshape(shp_k),
                             i.astype(jnp.float32).reshape(shp_v), log_f.reshape(shp_k),
                             S0.astype(jnp.float32))
    o = _rmsnorm(o, g_o) * jax.nn.silu(g.astype(jnp.float32)).reshape(shp_v)
    return o.reshape(B, T, hv).astype(xn.dtype) @ w_out, S_T


def _shared_kv(h, pos, g_in, w_dkv, g_kv):
    n = _rmsnorm(h, g_in)
    kv = n @ w_dkv
    c = _rmsnorm(kv[..., :KV_LORA], g_kv)
    kr = _rope(kv[..., KV_LORA:], pos)
    return c, kr


def _attend_prompt(q_abs, q_rope, c, kr):
    B, T, H, C = q_abs.shape
    nb = T // Q_BLOCK
    qa = q_abs.reshape(B, nb, Q_BLOCK, H, C).transpose(1, 0, 2, 3, 4)
    qr = q_rope.reshape(B, nb, Q_BLOCK, H, QK_ROPE).transpose(1, 0, 2, 3, 4)
    cf = c.astype(jnp.float32)
    krf = kr.astype(jnp.float32)
    key_pos = jnp.arange(T)

    def block(args):
        qa_b, qr_b, start = args
        s = (jnp.einsum('bqhc,bkc->bhqk', qa_b.astype(jnp.float32), cf)
             + jnp.einsum('bqhr,bkr->bhqk', qr_b.astype(jnp.float32), krf)) * SM_SCALE
        q_pos = start + jnp.arange(Q_BLOCK)
        mask = key_pos[None, :] <= q_pos[:, None]
        p = jax.nn.softmax(jnp.where(mask, s, NEG_INF), axis=-1)
        return jnp.einsum('bhqk,bkc->bqhc', p, cf)

    o = lax.map(block, (qa, qr, jnp.arange(nb) * Q_BLOCK))
    return o.transpose(1, 0, 2, 3, 4).reshape(B, T, H, C).astype(q_abs.dtype)


def _attend_sample(q_abs, q_rope, c, kr, cache_ckv, cache_krope, page_table):
    B, T, H, C = q_abs.shape
    P = page_table.shape[1] * PAGE_SIZE
    key_pos = jnp.arange(P + T)
    q_pos = P + jnp.arange(T)
    mask = key_pos[None, :] <= q_pos[:, None]

    def one(args):
        pt, qa, qr, cn, krn = args
        ck = jnp.concatenate([cache_ckv[pt].reshape(P, C).astype(jnp.float32),
                              cn.astype(jnp.float32)], axis=0)
        kk = jnp.concatenate([cache_krope[pt].reshape(P, QK_ROPE).astype(jnp.float32),
                              krn.astype(jnp.float32)], axis=0)
        s = (jnp.einsum('qhc,kc->hqk', qa.astype(jnp.float32), ck)
             + jnp.einsum('qhr,kr->hqk', qr.astype(jnp.float32), kk)) * SM_SCALE
        p = jax.nn.softmax(jnp.where(mask, s, NEG_INF), axis=-1)
        return jnp.einsum('hqk,kc->qhc', p, ck)

    o = lax.map(one, (page_table, q_abs, q_rope, c, kr))
    return o.astype(q_abs.dtype)


def _mla_mixer(xn, pos, c, kr, w_dq, g_q, w_uq, w_ukv, w_out, attend):
    B, T, _ = xn.shape
    cq = _rmsnorm(xn @ w_dq, g_q)
    q = (cq @ w_uq).reshape(B, T, MLA_HEADS, QK_NOPE + QK_ROPE)
    q_nope = q[..., :QK_NOPE]
    q_rope = _rope(q[..., QK_NOPE:], pos)
    w = w_ukv.reshape(KV_LORA, MLA_HEADS, QK_NOPE + V_DIM)
    q_abs = jnp.einsum('bthn,chn->bthc', q_nope, w[..., :QK_NOPE])
    o_lat = attend(q_abs, q_rope, c, kr)
    o = jnp.einsum('bthc,chv->bthv', o_lat, w[..., QK_NOPE:])
    return o.reshape(B, T, MLA_HEADS * V_DIM) @ w_out


def _swiglu(x, wg, wu, wd):
    return (jax.nn.silu(x @ wg) * (x @ wu)) @ wd


def _moe(x, w_r, wg, wu, wd):
    B, T, D = x.shape
    xf = x.reshape(B * T, D)
    logits = (xf @ w_r).astype(jnp.float32)
    top_v, top_i = lax.top_k(logits, TOP_K)
    gates = jax.nn.softmax(top_v, axis=-1)
    combine = jnp.einsum('nk,nke->ne', gates,
                         jax.nn.one_hot(top_i, N_EXPERTS, dtype=jnp.float32)).astype(x.dtype)
    out = jnp.zeros_like(xf)
    for e in range(N_EXPERTS):
        out = out + combine[:, e:e + 1] * _swiglu(xf, wg[e], wu[e], wd[e])
    return out.reshape(B, T, D)


def _trunk(x, pos, S0, attend, p):
    lb_all = jnp.cumsum(jax.nn.softmax(p["gamma_lb"].astype(jnp.float32), axis=0), axis=0)
    h = x
    c = kr = None
    new_S = []
    for layer in range(DEPTH):
        if layer < N_A_LAYERS:
            a = layer
            o, S = _hgrn2_mixer(_rmsnorm(h, p["g_mix_a"][a]), S0[a], p["w_in_a"][a], lb_all[a],
                                p["g_onorm_a"][a], p["w_out_a"][a])
            new_S.append(S)
        else:
            b = layer - N_A_LAYERS
            if b == 0:
                c, kr = _shared_kv(h, pos, p["g_kv_in"], p["w_dkv"], p["g_kv"])
            o = _mla_mixer(_rmsnorm(h, p["g_mix_b"][b]), pos, c, kr, p["w_dq"][b], p["g_q"][b],
                           p["w_uq"][b], p["w_ukv"], p["w_out_b"][b], attend)
        h = h + o
        hn = _rmsnorm(h, p["g_ffn"][layer])
        if layer % 2 == 0:
            d = layer // 2
            h = h + _swiglu(hn, p["w_ff_gate"][d], p["w_ff_up"][d], p["w_ff_down"][d])
        else:
            m = layer // 2
            h = h + _moe(hn, p["w_router"][m], p["w_e_gate"][m], p["w_e_up"][m], p["w_e_down"][m])
    y = _rmsnorm(h, p["g_final"])
    return y, c, kr, jnp.stack(new_S)


def setup_inputs(seed: int = 0) -> dict:
    key = jax.random.key(seed)
    ks = iter(jax.random.split(key, 48))

    def nrm(shape, scale):
        return jax.random.normal(next(ks), shape, jnp.float32) * scale

    def gain(shape):
        return 1.0 + 0.02 * jax.random.normal(next(ks), shape, jnp.float32)

    n_pages = PAST_LEN // PAGE_SIZE
    n_pool = (5 * DEC_BATCH * n_pages) // 4
    hk = HGRN_HEADS * HGRN_DK
    hv = HGRN_HEADS * HGRN_DV
    x_prompt = nrm((BATCH, SEQ, D_MODEL), 1.0)
    x_sample = nrm((DEC_BATCH, DEC_SEQ, D_MODEL), 1.0)
    cache_ckv = nrm((n_pool, PAGE_SIZE, KV_LORA), 1.0)
    cache_krope = nrm((n_pool, PAGE_SIZE, QK_ROPE), 1.0)
    state_hgrn = nrm((N_A_LAYERS, DEC_BATCH, HGRN_HEADS, HGRN_DK, HGRN_DV), 0.3)
    page_table = jax.random.permutation(next(ks), n_pool)[:DEC_BATCH * n_pages].reshape(
        DEC_BATCH, n_pages).astype(jnp.int32)
    return {
        "x_prompt": x_prompt,
        "x_sample": x_sample,
        "cache_ckv": cache_ckv,
        "cache_krope": cache_krope,
        "state_hgrn": state_hgrn,
        "page_table": page_table,
        "g_mix_a": gain((N_A_LAYERS, D_MODEL)),
        "w_in_a": nrm((N_A_LAYERS, D_MODEL, 2 * hk + 2 * hv), D_MODEL ** -0.5),
        "gamma_lb": nrm((DEPTH, hk), 0.1),
        "g_onorm_a": gain((N_A_LAYERS, HGRN_DV)),
        "w_out_a": nrm((N_A_LAYERS, hv, D_MODEL), hv ** -0.5),
        "g_kv_in": gain((D_MODEL,)),
        "w_dkv": nrm((D_MODEL, KV_LORA + QK_ROPE), D_MODEL ** -0.5),
        "g_kv": gain((KV_LORA,)),
        "w_ukv": nrm((KV_LORA, MLA_HEADS * (QK_NOPE + V_DIM)), KV_LORA ** -0.5),
        "g_mix_b": gain((N_B_LAYERS, D_MODEL)),
        "w_dq": nrm((N_B_LAYERS, D_MODEL, Q_LORA), D_MODEL ** -0.5),
        "g_q": gain((N_B_LAYERS, Q_LORA)),
        "w_uq": nrm((N_B_LAYERS, Q_LORA, MLA_HEADS * (QK_NOPE + QK_ROPE)), Q_LORA ** -0.5),
        "w_out_b": nrm((N_B_LAYERS, MLA_HEADS * V_DIM, D_MODEL), (MLA_HEADS * V_DIM) ** -0.5),
        "g_ffn": gain((DEPTH, D_MODEL)),
        "w_ff_gate": nrm((N_DENSE, D_MODEL, D_FF), D_MODEL ** -0.5),
        "w_ff_up": nrm((N_DENSE, D_MODEL, D_FF), D_MODEL ** -0.5),
        "w_ff_down": nrm((N_DENSE, D_FF, D_MODEL), D_FF ** -0.5),
        "w_router": nrm((N_MOE, D_MODEL, N_EXPERTS), D_MODEL ** -0.5),
        "w_e_gate": nrm((N_MOE, N_EXPERTS, D_MODEL, D_FF_EXPERT), D_MODEL ** -0.5),
        "w_e_up": nrm((N_MOE, N_EXPERTS, D_MODEL, D_FF_EXPERT), D_MODEL ** -0.5),
        "w_e_down": nrm((N_MOE, N_EXPERTS, D_FF_EXPERT, D_MODEL), D_FF_EXPERT ** -0.5),
        "g_final": gain((D_MODEL,)),
    }


def reference(x_prompt, x_sample, cache_ckv, cache_krope, state_hgrn, page_table,
              g_mix_a, w_in_a, gamma_lb, g_onorm_a, w_out_a,
              g_kv_in, w_dkv, g_kv, w_ukv,
              g_mix_b, w_dq, g_q, w_uq, w_out_b,
              g_ffn, w_ff_gate, w_ff_up, w_ff_down,
              w_router, w_e_gate, w_e_up, w_e_down, g_final):
    params = {
        "g_mix_a": g_mix_a, "w_in_a": w_in_a, "gamma_lb": gamma_lb,
        "g_onorm_a": g_onorm_a, "w_out_a": w_out_a,
        "g_kv_in": g_kv_in, "w_dkv": w_dkv, "g_kv": g_kv, "w_ukv": w_ukv,
        "g_mix_b": g_mix_b, "w_dq": w_dq, "g_q": g_q, "w_uq": w_uq, "w_out_b": w_out_b,
        "g_ffn": g_ffn, "w_ff_gate": w_ff_gate, "w_ff_up": w_ff_up, "w_ff_down": w_ff_down,
        "w_router": w_router, "w_e_gate": w_e_gate, "w_e_up": w_e_up, "w_e_down": w_e_down,
        "g_final": g_final,
    }
    b_p, t_p, _ = x_prompt.shape
    pos_p = jnp.arange(t_p, dtype=jnp.float32)
    S0_p = jnp.zeros((N_A_LAYERS, b_p, HGRN_HEADS, HGRN_DK, HGRN_DV), jnp.float32)
    y_prompt, ckv_prompt, krope_prompt, hgrn_prompt = _trunk(x_prompt, pos_p, S0_p, _attend_prompt, params)
    past = page_table.shape[1] * PAGE_SIZE
    pos_s = past + jnp.arange(x_sample.shape[1], dtype=jnp.float32)
    attend_s = functools.partial(_attend_sample, cache_ckv=cache_ckv, cache_krope=cache_krope,
                                 page_table=page_table)
    y_sample, ckv_sample, krope_sample, hgrn_sample = _trunk(x_sample, pos_s, state_hgrn, attend_s, params)
    hgrn_prompt = hgrn_prompt.astype(state_hgrn.dtype)
    hgrn_sample = hgrn_sample.astype(state_hgrn.dtype)
    return (y_prompt, y_sample, ckv_prompt, krope_prompt, ckv_sample, krope_sample, hgrn_prompt, hgrn_sample)
```

```python
import functools

import numpy as np
import jax
import jax.numpy as jnp
from jax import lax
from jax.experimental import pallas as pl
from jax.experimental.pallas import tpu as pltpu

F32 = jnp.float32
BF16 = jnp.bfloat16

RMS_EPS = 1e-6
ROPE_THETA = 10000.0
HGRN_HEADS = 8
HGRN_DK = 128
HGRN_CHUNK = 64
MLA_HEADS = 8
QK_NOPE = 128
QK_ROPE = 64
V_DIM = 128
PAGE_SIZE = 128
TOP_K = 2
SM_SCALE = (QK_NOPE + QK_ROPE) ** -0.5
NEG_INF = -1e30
LANE = 128
ROPE_PAD = LANE
VMEM_LIMIT = 56 * 1024 * 1024


def _params(sem):
    return pltpu.CompilerParams(dimension_semantics=sem, vmem_limit_bytes=VMEM_LIMIT)


def _dot(a, b):
    return jnp.dot(a, b, preferred_element_type=F32)


def _dot_nt(a, b):
    return lax.dot_general(a, b, (((1,), (1,)), ((), ())), preferred_element_type=F32)


def _dot_tn(a, b):
    return lax.dot_general(a, b, (((0,), (0,)), ((), ())), preferred_element_type=F32)


def _rms(x, g):
    return x * lax.rsqrt(jnp.mean(x * x, axis=-1, keepdims=True) + RMS_EPS) * g


def _sigmoid(x):
    return 1.0 / (1.0 + jnp.exp(-x))


def _hgrn_proj_kernel(layer, x_ref, g_ref, gam_ref, w_ref, q_ref, k_ref, lf_ref, i_ref, sg_ref):
    d = x_ref.shape[-1]
    xb = _rms(x_ref[...], g_ref[...]).astype(BF16)

    def proj(j):
        return _dot(xb, w_ref[:, j * d:(j + 1) * d])

    gam = gam_ref[...]
    e = jnp.exp(gam - jnp.max(gam, axis=0, keepdims=True))
    lb = jnp.sum(e[:layer + 1], axis=0, keepdims=True) / jnp.sum(e, axis=0, keepdims=True)

    q = proj(0)
    q_ref[...] = q * _sigmoid(q)
    z = proj(1)
    t = jnp.exp(-jnp.abs(z))
    r = 1.0 / (1.0 + t)
    pos = z >= 0
    sig_z = jnp.where(pos, r, t * r)
    sig_mz = jnp.where(pos, t * r, r)
    k_ref[...] = (1.0 - lb) * sig_mz
    lf_ref[...] = jnp.log(lb + (1.0 - lb) * sig_z)
    i_ref[...] = proj(2)
    g = proj(3)
    sg_ref[...] = g * _sigmoid(g)


def _hgrn_proj(x2, g_mix, gamma_lb, w_in_bf, layer, tm):
    n, d = x2.shape
    tm = min(tm, n)
    row = pl.BlockSpec((tm, d), lambda i: (i, 0))
    out = jax.ShapeDtypeStruct((n, d), F32)
    return pl.pallas_call(
        functools.partial(_hgrn_proj_kernel, layer),
        grid=(n // tm,),
        in_specs=[row,
                  pl.BlockSpec((1, d), lambda i: (0, 0)),
                  pl.BlockSpec(gamma_lb.shape, lambda i: (0, 0)),
                  pl.BlockSpec(w_in_bf.shape, lambda i: (0, 0))],
        out_specs=[row] * 5,
        out_shape=[out] * 5,
        compiler_params=_params(("parallel",)),
        name="hgrn_proj",
    )(x2, g_mix.reshape(1, d), gamma_lb, w_in_bf)


def _gla_constants(L):
    nl = int(np.log2(L))
    assert 2 ** nl == L
    t = np.arange(L)
    me = np.zeros((2 + nl, L, L), np.float32)
    me[0] = (t[None, :] <= t[:, None])
    me[1] = (t[None, :] > t[:, None])
    masks = np.zeros((1 + nl, L, L), np.float32)
    masks[0] = np.eye(L)
    for l in range(nl):
        m = L >> (l + 1)
        blk = t // (2 * m)
        r = blk * 2 * m + m
        upper = t >= r
        j = t[None, :]
        up_rows = (j >= r[:, None]) & (j <= t[:, None])
        lo_rows = (j > t[:, None]) & (j <= r[:, None] - 1)
        me[2 + l] = np.where(upper[:, None], up_rows, lo_rows)
        masks[1 + l] = (upper[:, None] & ~upper[None, :] & (blk[:, None] == blk[None, :]))
    return me.reshape((2 + nl) * L, L), masks, nl


def _gla_kernel(L, nl, has_s0, mm, *refs):
    if has_s0:
        (q_ref, k_ref, lf_ref, v_ref, sg_ref, go_ref, me_ref, mk_ref, s0_ref,
         og_ref, st_ref, s_scr) = refs
    else:
        (q_ref, k_ref, lf_ref, v_ref, sg_ref, go_ref, me_ref, mk_ref,
         og_ref, st_ref, s_scr) = refs
        s0_ref = None
    bb, tb, _ = q_ref.shape
    nchunk = tb // L
    ti = pl.program_id(2)

    @pl.when(ti == 0)
    def _():
        if has_s0:
            for b in range(bb):
                s_scr[b] = s0_ref[b].T
        else:
            s_scr[...] = jnp.zeros_like(s_scr)

    me = me_ref[...]
    go = go_ref[...]

    def chunk(b, c):
        sl = pl.ds(pl.multiple_of(c * L, L), L)
        q = q_ref[b, sl, :]
        k = k_ref[b, sl, :]
        g = lf_ref[b, sl, :]
        v = v_ref[b, sl, :].astype(mm)
        st = s_scr[b]
        if mm == BF16:
            g1 = g.astype(BF16)
            r1 = g - g1.astype(F32)
            g2 = r1.astype(BF16)
            g3 = (r1 - g2.astype(F32)).astype(BF16)
            ee = _dot(me, g1) + _dot(me, g2) + _dot(me, g3)
        else:
            ee = _dot(me, g)
        b_cum = ee[0:L]
        o = _dot_nt((q * jnp.exp(b_cum)).astype(mm), st.astype(mm))
        sc = _dot_nt(q.astype(mm), k.astype(mm)) * mk_ref[0]
        for l in range(nl):
            e = jnp.exp(ee[(2 + l) * L:(3 + l) * L])
            sc = sc + _dot_nt((q * e).astype(mm), (k * e).astype(mm)) * mk_ref[1 + l]
        o = o + _dot(sc.astype(mm), v)
        kd = (k * jnp.exp(ee[L:2 * L])).astype(mm)
        s_scr[b] = st * jnp.exp(b_cum[L - 1:L, :]) + _dot_tn(v, kd)
        og = _rms(o, go) * sg_ref[b, sl, :]
        og_ref[b, sl, :] = og.astype(og_ref.dtype)

    def body(i, carry):
        chunk(i // nchunk, i % nchunk)
        return carry

    lax.fori_loop(0, bb * nchunk, body, 0)

    @pl.when(ti == pl.num_programs(2) - 1)
    def _():
        for b in range(bb):
            st_ref[b] = s_scr[b].T


def _gla(q, k, lf, v, sg, g_onorm, s0, bb, tb):
    B, T, hd = q.shape
    H = hd // HGRN_DK
    L = HGRN_CHUNK if T % HGRN_CHUNK == 0 else T
    tb = min(tb, T)
    bb = min(bb, B)
    mm = BF16 if L % 16 == 0 else F32
    me_np, mk_np, nl = _gla_constants(L)
    me = jnp.asarray(me_np, mm)
    mk = jnp.asarray(mk_np, F32)
    seq = pl.BlockSpec((bb, tb, HGRN_DK), lambda b, h, t: (b, t, h))
    st_spec = pl.BlockSpec((bb, None, HGRN_DK, HGRN_DK), lambda b, h, t: (b, h, 0, 0))
    in_specs = [seq] * 5 + [
        pl.BlockSpec((1, HGRN_DK), lambda b, h, t: (0, 0)),
        pl.BlockSpec(me.shape, lambda b, h, t: (0, 0)),
        pl.BlockSpec(mk.shape, lambda b, h, t: (0, 0, 0)),
    ]
    args = [q, k, lf, v, sg, g_onorm.reshape(1, HGRN_DK), me, mk]
    if s0 is not None:
        in_specs.append(st_spec)
        args.append(s0)
    return pl.pallas_call(
        functools.partial(_gla_kernel, L, nl, s0 is not None, mm),
        grid=(B // bb, H, T // tb),
        in_specs=in_specs,
        out_specs=[seq, st_spec],
        out_shape=[jax.ShapeDtypeStruct((B, T, hd), BF16),
                   jax.ShapeDtypeStruct((B, H, HGRN_DK, HGRN_DK), F32)],
        scratch_shapes=[pltpu.VMEM((bb, HGRN_DK, HGRN_DK), F32)],
        compiler_params=_params(("parallel", "parallel", "arbitrary")),
        name="gla",
    )(*args)


def _ffn_kernel(h_ref, og_ref, wo_ref, g_ref, wg_ref, wu_ref, wd_ref, out_ref,
                h1_scr, hn_scr, acc_scr):
    f = pl.program_id(1)

    @pl.when(f == 0)
    def _():
        h1 = h_ref[...] + _dot(og_ref[...], wo_ref[...])
        h1_scr[...] = h1
        hn_scr[...] = _rms(h1, g_ref[...]).astype(BF16)
        acc_scr[...] = jnp.zeros_like(acc_scr)

    hn = hn_scr[...]
    a = _dot(hn, wg_ref[...])
    u = _dot(hn, wu_ref[...])
    acc_scr[...] += _dot((a * _sigmoid(a) * u).astype(BF16), wd_ref[...])

    @pl.when(f == pl.num_programs(1) - 1)
    def _():
        out_ref[...] = h1_scr[...] + acc_scr[...]


def _pick_tf(dff, target):
    best = None
    for tf in range(LANE, dff + 1, LANE):
        if dff % tf == 0 and tf <= target:
            best = tf
    return best or dff


def _ffn(h, og, w_out_bf, g_ffn, wg_bf, wu_bf, wd_bf, tm, tf_target):
    n, d = h.shape
    dff = wg_bf.shape[1]
    tm = min(tm, n)
    tf = _pick_tf(dff, tf_target)
    row = lambda i, f: (i, 0)
    return pl.pallas_call(
        _ffn_kernel,
        grid=(n // tm, dff // tf),
        in_specs=[pl.BlockSpec((tm, d), row),
                  pl.BlockSpec((tm, d), row),
                  pl.BlockSpec((d, d), lambda i, f: (0, 0)),
                  pl.BlockSpec((1, d), lambda i, f: (0, 0)),
                  pl.BlockSpec((d, tf), lambda i, f: (0, f)),
                  pl.BlockSpec((d, tf), lambda i, f: (0, f)),
                  pl.BlockSpec((tf, d), lambda i, f: (f, 0))],
        out_specs=pl.BlockSpec((tm, d), row),
        out_shape=jax.ShapeDtypeStruct((n, d), F32),
        scratch_shapes=[pltpu.VMEM((tm, d), F32), pltpu.VMEM((tm, d), BF16),
                        pltpu.VMEM((tm, d), F32)],
        compiler_params=_params(("parallel", "arbitrary")),
        name="ffn",
    )(h, og, w_out_bf, g_ffn.reshape(1, d), wg_bf, wu_bf, wd_bf)


def _kvq_kernel(kv_lora, h_ref, cos_ref, sin_ref, gkin_ref, wkv_ref, wkvs_ref, gkv_ref,
                gmix_ref, wdq_ref, gq_ref, wqn_ref, wqr_ref, wqrs_ref, wuk_ref,
                c_ref, kr_ref, kcat_ref, qcat_ref):
    h = h_ref[...]
    cos = cos_ref[...]
    sin = sin_ref[...]
    nb = _rms(h, gkin_ref[...]).astype(BF16)
    kv = _dot(nb, wkv_ref[...])
    kvs = _dot(nb, wkvs_ref[...])
    c = _rms(kv[:, :kv_lora], gkv_ref[...])
    kr = kv[:, kv_lora:] * cos + kvs * sin
    c_ref[...] = c
    kr_ref[...] = kr[:, :QK_ROPE]
    kcat_ref[:, :kv_lora] = c.astype(BF16)
    kcat_ref[:, kv_lora:] = kr.astype(BF16)
    xb = _rms(h, gmix_ref[...]).astype(BF16)
    cq = _rms(_dot(xb, wdq_ref[...]), gq_ref[...]).astype(BF16)
    qn = _dot(cq, wqn_ref[...])
    qr = _dot(cq, wqr_ref[...])
    qrs = _dot(cq, wqrs_ref[...])
    for hh in range(MLA_HEADS):
        qn_h = qn[:, hh * QK_NOPE:(hh + 1) * QK_NOPE].astype(BF16)
        qa_h = _dot(qn_h, wuk_ref[hh]) * SM_SCALE
        sl = slice(hh * ROPE_PAD, (hh + 1) * ROPE_PAD)
        qr_h = (qr[:, sl] * cos + qrs[:, sl] * sin) * SM_SCALE
        qcat_ref[hh, :, :kv_lora] = qa_h.astype(qcat_ref.dtype)
        qcat_ref[hh, :, kv_lora:] = qr_h.astype(qcat_ref.dtype)


def _kvq(h2, cos_t, sin_t, p, q_dtype, tm):
    n, d = h2.shape
    tm = min(tm, cos_t.shape[0])
    kv_lora = p["g_kv"].shape[-1]
    kcw = kv_lora + ROPE_PAD
    nt = cos_t.shape[0] // tm
    const = lambda a: pl.BlockSpec(a.shape, lambda i: (0,) * a.ndim)
    row = lambda w: pl.BlockSpec((tm, w), lambda i: (i, 0))
    tab = pl.BlockSpec((tm, ROPE_PAD), lambda i: (i % nt, 0))
    weights = [p["g_kv_in"], p["w_kv"], p["w_kvs"], p["g_kv"], p["g_mix_b"], p["w_dq"],
               p["g_q"], p["w_qn"], p["w_qr"], p["w_qrs"], p["w_uk"]]
    return pl.pallas_call(
        functools.partial(_kvq_kernel, kv_lora),
        grid=(n // tm,),
        in_specs=[row(d), tab, tab] + [const(w) for w in weights],
        out_specs=[row(kv_lora), row(QK_ROPE), row(kcw),
                   pl.BlockSpec((MLA_HEADS, tm, kcw), lambda i: (0, i, 0))],
        out_shape=[jax.ShapeDtypeStruct((n, kv_lora), F32),
                   jax.ShapeDtypeStruct((n, QK_ROPE), F32),
                   jax.ShapeDtypeStruct((n, kcw), BF16),
                   jax.ShapeDtypeStruct((MLA_HEADS, n, kcw), q_dtype)],
        compiler_params=_params(("parallel",)),
        name="kvq",
    )(h2, cos_t, sin_t, *weights)


def _attn_prompt_kernel(tq, tk, kv_lora, qi_ref, kj_ref, first_ref, last_ref,
                        q_ref, k_ref, o_ref, m_scr, l_scr, acc_scr):
    p_id = pl.program_id(1)
    qi = qi_ref[p_id]
    kj = kj_ref[p_id]
    rows = MLA_HEADS * tq

    @pl.when(first_ref[p_id] == 1)
    def _():
        m_scr[...] = jnp.full_like(m_scr, NEG_INF)
        l_scr[...] = jnp.zeros_like(l_scr)
        acc_scr[...] = jnp.zeros_like(acc_scr)

    q = q_ref[...].reshape(rows, q_ref.shape[-1])
    k = k_ref[...]
    s = _dot_nt(q, k)
    q_pos = qi * tq + lax.broadcasted_iota(jnp.int32, (rows, tk), 0) % tq
    k_pos = kj * tk + lax.broadcasted_iota(jnp.int32, (rows, tk), 1)
    s = jnp.where(k_pos <= q_pos, s, NEG_INF)
    m_prev = m_scr[...]
    m_new = jnp.maximum(m_prev, jnp.max(s, axis=-1, keepdims=True))
    alpha = jnp.exp(m_prev - m_new)
    pr = jnp.exp(s - m_new)
    l_scr[...] = alpha * l_scr[...] + jnp.sum(pr, axis=-1, keepdims=True)
    acc_scr[...] = alpha * acc_scr[...] + _dot(pr.astype(BF16), k[:, :kv_lora])
    m_scr[...] = m_new

    @pl.when(last_ref[p_id] == 1)
    def _():
        o = acc_scr[...] / l_scr[...]
        o_ref[...] = o.reshape(MLA_HEADS, tq, kv_lora).astype(o_ref.dtype)


def _attn_prompt(qcat, kcat, B, T, kv_lora, tq, tk):
    tq = min(tq, T)
    tk = min(tk, T)
    nq, nk = T // tq, T // tk
    qi, kj, first, last = [], [], [], []
    for i in range(nq):
        need = -(-((i + 1) * tq) // tk)
        for j in range(need):
            qi.append(i); kj.append(j); first.append(int(j == 0)); last.append(int(j == need - 1))
    tabs = [jnp.asarray(np.array(a, np.int32)) for a in (qi, kj, first, last)]
    kcw = kcat.shape[-1]
    rows = MLA_HEADS * tq
    grid_spec = pltpu.PrefetchScalarGridSpec(
        num_scalar_prefetch=4,
        grid=(B, len(qi)),
        in_specs=[pl.BlockSpec((MLA_HEADS, tq, kcw), lambda b, p, qi, kj, fi, la: (0, b * nq + qi[p], 0)),
                  pl.BlockSpec((tk, kcw), lambda b, p, qi, kj, fi, la: (b * nk + kj[p], 0))],
        out_specs=pl.BlockSpec((MLA_HEADS, tq, kv_lora),
                               lambda b, p, qi, kj, fi, la: (0, b * nq + qi[p], 0)),
        scratch_shapes=[pltpu.VMEM((rows, 1), F32), pltpu.VMEM((rows, 1), F32),
                        pltpu.VMEM((rows, kv_lora), F32)],
    )
    return pl.pallas_call(
        functools.partial(_attn_prompt_kernel, tq, tk, kv_lora),
        grid_spec=grid_spec,
        out_shape=jax.ShapeDtypeStruct((MLA_HEADS, B * T, kv_lora), BF16),
        compiler_params=_params(("parallel", "arbitrary")),
        name="attn_prompt",
    )(*tabs, qcat, kcat)


def _attn_sample_kernel(pps, kv_lora, pt_ref, q_ref, cn_ref, krn_ref, *refs):
    c_refs = refs[:pps]
    kr_refs = refs[pps:2 * pps]
    o_ref = refs[2 * pps]
    kc_scr, kk_scr, m_scr, l_scr, acc_scr = refs[2 * pps + 1:]
    g = pl.program_id(1)
    t_new = q_ref.shape[1]
    rows = MLA_HEADS * t_new
    q = q_ref[...].reshape(rows, q_ref.shape[-1])
    qa = q[:, :kv_lora]
    qr = q[:, kv_lora:kv_lora + QK_ROPE]

    @pl.when(g == 0)
    def _():
        cn = cn_ref[...]
        s = _dot_nt(qa, cn) + _dot_nt(qr, krn_ref[...])
        q_t = lax.broadcasted_iota(jnp.int32, s.shape, 0) % t_new
        k_t = lax.broadcasted_iota(jnp.int32, s.shape, 1)
        s = jnp.where(k_t <= q_t, s, NEG_INF)
        m = jnp.max(s, axis=-1, keepdims=True)
        pr = jnp.exp(s - m)
        m_scr[...] = m
        l_scr[...] = jnp.sum(pr, axis=-1, keepdims=True)
        acc_scr[...] = _dot(pr, cn)

    for j in range(pps):
        kc_scr[j * PAGE_SIZE:(j + 1) * PAGE_SIZE, :] = c_refs[j][...].astype(BF16)
        kk_scr[j * PAGE_SIZE:(j + 1) * PAGE_SIZE, :] = kr_refs[j][...].astype(BF16)
    kc = kc_scr[...]
    s = _dot_nt(qa.astype(BF16), kc) + _dot_nt(qr.astype(BF16), kk_scr[...])
    m_prev = m_scr[...]
    m_new = jnp.maximum(m_prev, jnp.max(s, axis=-1, keepdims=True))
    alpha = jnp.exp(m_prev - m_new)
    pr = jnp.exp(s - m_new)
    l_scr[...] = alpha * l_scr[...] + jnp.sum(pr, axis=-1, keepdims=True)
    acc_scr[...] = alpha * acc_scr[...] + _dot(pr.astype(BF16), kc)
    m_scr[...] = m_new

    @pl.when(g == pl.num_programs(1) - 1)
    def _():
        o = acc_scr[...] / l_scr[...]
        o_ref[...] = o.reshape(MLA_HEADS, t_new, kv_lora)


def _attn_sample(qcat, c_new, kr_new, cache_ckv, cache_krope, page_table, pps):
    B, n_pages = page_table.shape
    t_new = c_new.shape[0] // B
    kv_lora = c_new.shape[-1]
    kcw = qcat.shape[-1]
    pps = min(pps, n_pages)
    while n_pages % pps:
        pps -= 1
    rows = MLA_HEADS * t_new
    pt_flat = page_table.reshape(-1)

    def page_map(j):
        return lambda b, g, pt: (pt[b * n_pages + g * pps + j], 0, 0)

    grid_spec = pltpu.PrefetchScalarGridSpec(
        num_scalar_prefetch=1,
        grid=(B, n_pages // pps),
        in_specs=[pl.BlockSpec((MLA_HEADS, t_new, kcw), lambda b, g, pt: (0, b, 0)),
                  pl.BlockSpec((t_new, kv_lora), lambda b, g, pt: (b, 0)),
                  pl.BlockSpec((t_new, QK_ROPE), lambda b, g, pt: (b, 0))]
                 + [pl.BlockSpec((None, PAGE_SIZE, kv_lora), page_map(j)) for j in range(pps)]
                 + [pl.BlockSpec((None, PAGE_SIZE, QK_ROPE), page_map(j)) for j in range(pps)],
        out_specs=pl.BlockSpec((MLA_HEADS, t_new, kv_lora), lambda b, g, pt: (0, b, 0)),
        scratch_shapes=[pltpu.VMEM((pps * PAGE_SIZE, kv_lora), BF16),
                        pltpu.VMEM((pps * PAGE_SIZE, QK_ROPE), BF16),
                        pltpu.VMEM((rows, 1), F32), pltpu.VMEM((rows, 1), F32),
                        pltpu.VMEM((rows, kv_lora), F32)],
    )
    return pl.pallas_call(
        functools.partial(_attn_sample_kernel, pps, kv_lora),
        grid_spec=grid_spec,
        out_shape=jax.ShapeDtypeStruct((MLA_HEADS, B * t_new, kv_lora), F32),
        compiler_params=_params(("parallel", "arbitrary")),
        name="attn_sample",
    )(pt_flat, qcat, c_new, kr_new, *([cache_ckv] * pps), *([cache_krope] * pps))


def _moe_kernel(n_exp, h_ref, o_ref, wuv_ref, wo_ref, g_ref, wr_ref, gfin_ref,
                wg_ref, wu_ref, wd_ref, out_ref, h3_scr, hn_scr, comb_scr, acc_scr):
    e = pl.program_id(1)
    f = pl.program_id(2)
    tm = h_ref.shape[0]

    @pl.when((e == 0) & (f == 0))
    def _():
        heads = [_dot(o_ref[hh].astype(BF16), wuv_ref[hh]).astype(BF16) for hh in range(MLA_HEADS)]
        h3 = h_ref[...] + _dot(jnp.concatenate(heads, axis=-1), wo_ref[...])
        h3_scr[...] = h3
        hn = _rms(h3, g_ref[...])
        hn_scr[...] = hn.astype(BF16)
        acc_scr[...] = jnp.zeros_like(acc_scr)
        lg = jnp.dot(hn, wr_ref[...], preferred_element_type=F32, precision=lax.Precision.HIGHEST)
        lane = lax.broadcasted_iota(jnp.int32, lg.shape, 1)
        lg = jnp.where(lane < n_exp, lg, -jnp.inf)
        m1 = jnp.max(lg, axis=-1, keepdims=True)
        i1 = jnp.min(jnp.where(lg == m1, lane, LANE), axis=-1, keepdims=True)
        lg2 = jnp.where(lane == i1, -jnp.inf, lg)
        m2 = jnp.max(lg2, axis=-1, keepdims=True)
        i2 = jnp.min(jnp.where(lg2 == m2, lane, LANE), axis=-1, keepdims=True)
        t = jnp.exp(m2 - m1)
        g1 = 1.0 / (1.0 + t)
        comb_scr[...] = jnp.where(lane == i1, g1, 0.0) + jnp.where(lane == i2, t * g1, 0.0)

    hn = hn_scr[...]
    a = _dot(hn, wg_ref[...])
    u = _dot(hn, wu_ref[...])
    y = _dot((a * _sigmoid(a) * u).astype(BF16), wd_ref[...])
    lane = lax.broadcasted_iota(jnp.int32, (tm, LANE), 1)
    ce = jnp.sum(jnp.where(lane == e, comb_scr[...], 0.0), axis=-1, keepdims=True)
    acc_scr[...] += ce * y

    @pl.when((e == pl.num_programs(1) - 1) & (f == pl.num_programs(2) - 1))
    def _():
        out_ref[...] = _rms(h3_scr[...] + acc_scr[...], gfin_ref[...])


def _moe(h2, o_lat, p, tm, tf_target):
    n, d = h2.shape
    n_exp, _, dff = p["w_e_gate"].shape
    tm = min(tm, n)
    tf = _pick_tf(dff, tf_target)
    kv_lora = o_lat.shape[-1]
    c2 = lambda i, e, f: (0, 0)
    c3 = lambda i, e, f: (0, 0, 0)
    return pl.pallas_call(
        functools.partial(_moe_kernel, n_exp),
        grid=(n // tm, n_exp, dff // tf),
        in_specs=[pl.BlockSpec((tm, d), lambda i, e, f: (i, 0)),
                  pl.BlockSpec((MLA_HEADS, tm, kv_lora), lambda i, e, f: (0, i, 0)),
                  pl.BlockSpec(p["w_uv"].shape, c3),
                  pl.BlockSpec((d, d), c2),
                  pl.BlockSpec((1, d), c2),
                  pl.BlockSpec((d, LANE), c2),
                  pl.BlockSpec((1, d), c2),
                  pl.BlockSpec((None, d, tf), lambda i, e, f: (e, 0, f)),
                  pl.BlockSpec((None, d, tf), lambda i, e, f: (e, 0, f)),
                  pl.BlockSpec((None, tf, d), lambda i, e, f: (e, f, 0))],
        out_specs=pl.BlockSpec((tm, d), lambda i, e, f: (i, 0)),
        out_shape=jax.ShapeDtypeStruct((n, d), F32),
        scratch_shapes=[pltpu.VMEM((tm, d), F32), pltpu.VMEM((tm, d), BF16),
                        pltpu.VMEM((tm, LANE), F32), pltpu.VMEM((tm, d), F32)],
        compiler_params=_params(("parallel", "arbitrary", "arbitrary")),
        name="moe",
    )(h2, o_lat, p["w_uv"], p["w_out_b"], p["g_ffn1"], p["w_router"], p["g_final"],
      p["w_e_gate"], p["w_e_up"], p["w_e_down"])


def _rope_tables(pos):
    half = QK_ROPE // 2
    inv = ROPE_THETA ** (-jnp.arange(half, dtype=F32) / half)
    ang = pos.astype(F32)[:, None] * inv[None, :]
    cos, sin = jnp.cos(ang), jnp.sin(ang)
    z = jnp.zeros((pos.shape[0], ROPE_PAD - QK_ROPE), F32)
    return (jnp.concatenate([cos, cos, z], axis=-1), jnp.concatenate([-sin, sin, z], axis=-1))


def _swap_halves(w):
    half = w.shape[-1] // 2
    return jnp.concatenate([w[..., half:], w[..., :half]], axis=-1)


def _pad_rope(w):
    return jnp.pad(w, [(0, 0)] * (w.ndim - 1) + [(0, ROPE_PAD - QK_ROPE)])


def _prepare(g_mix_a, w_in_a, gamma_lb, g_onorm_a, w_out_a, g_kv_in, w_dkv, g_kv, w_ukv,
             g_mix_b, w_dq, g_q, w_uq, w_out_b, g_ffn, w_ff_gate, w_ff_up, w_ff_down,
             w_router, w_e_gate, w_e_up, w_e_down, g_final):
    d = g_final.shape[-1]
    kv_lora = g_kv.shape[-1]
    q_lora = g_q.shape[-1]
    row = lambda g: g.reshape(1, -1)
    wk_rope = w_dkv[:, kv_lora:]
    w_uq3 = w_uq[0].reshape(q_lora, MLA_HEADS, QK_NOPE + QK_ROPE)
    wq_rope = w_uq3[..., QK_NOPE:]
    w_ukv3 = w_ukv.reshape(kv_lora, MLA_HEADS, QK_NOPE + V_DIM)
    n_exp = w_router.shape[-1]
    return {
        "g_mix_a": g_mix_a[0], "w_in": w_in_a[0].astype(BF16), "gamma_lb": gamma_lb,
        "g_onorm": g_onorm_a[0], "w_out_a": w_out_a[0].astype(BF16),
        "g_ffn0": g_ffn[0], "w_ff_gate": w_ff_gate[0].astype(BF16),
        "w_ff_up": w_ff_up[0].astype(BF16), "w_ff_down": w_ff_down[0].astype(BF16),
        "g_kv_in": row(g_kv_in), "g_kv": row(g_kv), "g_mix_b": row(g_mix_b[0]), "g_q": row(g_q[0]),
        "w_kv": jnp.concatenate([w_dkv[:, :kv_lora], _pad_rope(wk_rope)], axis=-1).astype(BF16),
        "w_kvs": _pad_rope(_swap_halves(wk_rope)).astype(BF16),
        "w_dq": w_dq[0].astype(BF16),
        "w_qn": w_uq3[..., :QK_NOPE].reshape(q_lora, -1).astype(BF16),
        "w_qr": _pad_rope(wq_rope).reshape(q_lora, -1).astype(BF16),
        "w_qrs": _pad_rope(_swap_halves(wq_rope)).reshape(q_lora, -1).astype(BF16),
        "w_uk": jnp.transpose(w_ukv3[..., :QK_NOPE], (1, 2, 0)).astype(BF16),
        "w_uv": jnp.transpose(w_ukv3[..., QK_NOPE:], (1, 0, 2)).astype(BF16),
        "w_out_b": w_out_b[0].astype(BF16),
        "g_ffn1": row(g_ffn[1]), "g_final": row(g_final),
        "w_router": jnp.pad(w_router[0], ((0, 0), (0, LANE - n_exp))),
        "w_e_gate": w_e_gate[0].astype(BF16), "w_e_up": w_e_up[0].astype(BF16),
        "w_e_down": w_e_down[0].astype(BF16),
    }


def _trunk(x, pos, s0, p, attend, q_dtype, gla_bb, tm):
    B, T, d = x.shape
    n = B * T
    x2 = x.reshape(n, d)
    q, k, lf, v, sg = _hgrn_proj(x2, p["g_mix_a"], p["gamma_lb"], p["w_in"], 0, tm)
    r3 = lambda a: a.reshape(B, T, d)
    og, s_t = _gla(r3(q), r3(k), r3(lf), r3(v), r3(sg), p["g_onorm"], s0, gla_bb, 512)
    h2 = _ffn(x2, og.reshape(n, d), p["w_out_a"], p["g_ffn0"], p["w_ff_gate"], p["w_ff_up"],
              p["w_ff_down"], tm, 1408)
    cos_t, sin_t = _rope_tables(pos)
    c, kr, kcat, qcat = _kvq(h2, cos_t, sin_t, p, q_dtype, tm)
    o_lat = attend(qcat, kcat, c, kr)
    y = _moe(h2, o_lat, p, tm, 1408)
    kv_lora = c.shape[-1]
    return (y.reshape(B, T, d), c.reshape(B, T, kv_lora), kr.reshape(B, T, QK_ROPE), s_t[None])


def kernel(x_prompt, x_sample, cache_ckv, cache_krope, state_hgrn, page_table, g_mix_a, w_in_a,
           gamma_lb, g_onorm_a, w_out_a, g_kv_in, w_dkv, g_kv, w_ukv, g_mix_b, w_dq, g_q, w_uq,
           w_out_b, g_ffn, w_ff_gate, w_ff_up, w_ff_down, w_router, w_e_gate, w_e_up, w_e_down,
           g_final):
    p = _prepare(g_mix_a, w_in_a, gamma_lb, g_onorm_a, w_out_a, g_kv_in, w_dkv, g_kv, w_ukv,
                 g_mix_b, w_dq, g_q, w_uq, w_out_b, g_ffn, w_ff_gate, w_ff_up, w_ff_down,
                 w_router, w_e_gate, w_e_up, w_e_down, g_final)
    kv_lora = g_kv.shape[-1]
    bp, tp, _ = x_prompt.shape
    bs, ts, _ = x_sample.shape

    def attend_p(qcat, kcat, c, kr):
        return _attn_prompt(qcat, kcat, bp, tp, kv_lora, 128, 512)

    y_p, c_p, kr_p, s_p = _trunk(x_prompt, jnp.arange(tp, dtype=F32), None, p, attend_p,
                                 BF16, 1, 512)

    past = page_table.shape[1] * PAGE_SIZE
    pos_s = jnp.tile(past + jnp.arange(ts, dtype=F32), bs)

    def attend_s(qcat, kcat, c, kr):
        return _attn_sample(qcat, c, kr, cache_ckv, cache_krope, page_table, 16)

    y_s, c_s, kr_s, s_s = _trunk(x_sample, pos_s, state_hgrn[0], p, attend_s, F32, bs, 512)
    return (y_p, y_s, c_p, kr_p, c_s, kr_s, s_p.astype(state_hgrn.dtype),
            s_s.astype(state_hgrn.dtype))
```

```python
import functools

import numpy as np
import jax
import jax.numpy as jnp
from jax import lax
from jax.experimental import pallas as pl
from jax.experimental.pallas import tpu as pltpu

F32 = jnp.float32
BF16 = jnp.bfloat16

RMS_EPS = 1e-6
ROPE_THETA = 10000.0
HGRN_HEADS = 8
HGRN_DK = 128
HGRN_CHUNK = 64
MLA_HEADS = 8
QK_NOPE = 128
QK_ROPE = 64
V_DIM = 128
PAGE_SIZE = 128
TOP_K = 2
SM_SCALE = (QK_NOPE + QK_ROPE) ** -0.5
NEG_INF = -1e30
LANE = 128
SUBLANE = 8
ROPE_PAD = LANE
VMEM_LIMIT = 56 * 1024 * 1024


def _params(sem):
    return pltpu.CompilerParams(dimension_semantics=sem, vmem_limit_bytes=VMEM_LIMIT)


def _dot(a, b):
    return jnp.dot(a, b, preferred_element_type=F32)


def _dot_nt(a, b):
    return lax.dot_general(a, b, (((1,), (1,)), ((), ())), preferred_element_type=F32)


def _dot_tn(a, b):
    return lax.dot_general(a, b, (((0,), (0,)), ((), ())), preferred_element_type=F32)


def _rms(x, g):
    return x * lax.rsqrt(jnp.mean(x * x, axis=-1, keepdims=True) + RMS_EPS) * g


def _sigmoid(x):
    return 1.0 / (1.0 + jnp.exp(-x))


def _lane_fold(x, op):
    w = x.shape[-1]
    if w % LANE or w == LANE:
        return x
    acc = x[:, :LANE]
    for j in range(1, w // LANE):
        acc = op(acc, x[:, j * LANE:(j + 1) * LANE])
    return acc


def _hgrn_proj_kernel(layer, x_ref, g_ref, gam_ref, w_ref, q_ref, k_ref, lf_ref, i_ref, sg_ref):
    d = x_ref.shape[-1]
    xb = _rms(x_ref[...], g_ref[...]).astype(BF16)

    def proj(j):
        return _dot(xb, w_ref[:, j * d:(j + 1) * d])

    gam = gam_ref[...]
    e = jnp.exp(gam - jnp.max(gam, axis=0, keepdims=True))
    lb = jnp.sum(e[:layer + 1], axis=0, keepdims=True) / jnp.sum(e, axis=0, keepdims=True)

    q = proj(0)
    q_ref[...] = q * _sigmoid(q)
    z = proj(1)
    t = jnp.exp(-jnp.abs(z))
    r = 1.0 / (1.0 + t)
    pos = z >= 0
    sig_z = jnp.where(pos, r, t * r)
    sig_mz = jnp.where(pos, t * r, r)
    k_ref[...] = (1.0 - lb) * sig_mz
    lf_ref[...] = jnp.log(lb + (1.0 - lb) * sig_z)
    i_ref[...] = proj(2)
    g = proj(3)
    sg_ref[...] = g * _sigmoid(g)


def _hgrn_proj(x2, g_mix, gamma_lb, w_in_bf, layer, tm):
    n, d = x2.shape
    tm = min(tm, n)
    row = pl.BlockSpec((tm, d), lambda i: (i, 0))
    out = jax.ShapeDtypeStruct((n, d), F32)
    return pl.pallas_call(
        functools.partial(_hgrn_proj_kernel, layer),
        grid=(n // tm,),
        in_specs=[row,
                  pl.BlockSpec((1, d), lambda i: (0, 0)),
                  pl.BlockSpec(gamma_lb.shape, lambda i: (0, 0)),
                  pl.BlockSpec(w_in_bf.shape, lambda i: (0, 0))],
        out_specs=[row] * 5,
        out_shape=[out] * 5,
        compiler_params=_params(("parallel",)),
        name="hgrn_proj",
    )(x2, g_mix.reshape(1, d), gamma_lb, w_in_bf)


def _gla_constants(L):
    nl = int(np.log2(L))
    assert 2 ** nl == L
    t = np.arange(L)
    me = np.zeros((1 + nl, L, L), np.float32)
    me[0] = (t[None, :] <= t[:, None])
    masks = np.zeros((1 + nl, L, L), np.float32)
    masks[0] = np.eye(L)
    for l in range(nl):
        m = L >> (l + 1)
        blk = t // (2 * m)
        r = blk * 2 * m + m
        upper = t >= r
        j = t[None, :]
        up_rows = (j >= r[:, None]) & (j <= t[:, None])
        lo_rows = (j > t[:, None]) & (j <= r[:, None] - 1)
        me[1 + l] = np.where(upper[:, None], up_rows, lo_rows)
        masks[1 + l] = (upper[:, None] & ~upper[None, :] & (blk[:, None] == blk[None, :]))
    return me.reshape((1 + nl) * L, L), masks, nl


def _gla_kernel(L, nl, nh, has_s0, mm, *refs):
    s_scrs = refs[-nh:]
    refs = refs[:-nh]
    if has_s0:
        (q_ref, k_ref, lf_ref, v_ref, sg_ref, go_ref, me_ref, mk_ref, s0_ref,
         og_ref, st_ref) = refs
    else:
        (q_ref, k_ref, lf_ref, v_ref, sg_ref, go_ref, me_ref, mk_ref,
         og_ref, st_ref) = refs
        s0_ref = None
    bb, tb, _ = q_ref.shape
    nchunk = tb // L
    ti = pl.program_id(1)
    dk = HGRN_DK

    @pl.when(ti == 0)
    def _():
        if has_s0:
            def init(b, carry):
                for h in range(nh):
                    s_scrs[h][b] = s0_ref[b, h].T
                return carry
            lax.fori_loop(0, bb, init, 0)
        else:
            for h in range(nh):
                s_scrs[h][...] = jnp.zeros_like(s_scrs[h])

    me = me_ref[...]
    go = go_ref[...]

    def chunk(b, c):
        sl = pl.ds(pl.multiple_of(c * L, L), L)
        g = lf_ref[b, sl, :]
        if mm == BF16:
            g1 = g.astype(BF16)
            g2 = (g - g1.astype(F32)).astype(BF16)
            ee = _dot(me, g1) + _dot(me, g2)
        else:
            ee = _dot(me, g)
        for h in range(nh):
            hs = slice(h * dk, (h + 1) * dk)
            q = q_ref[b, sl, hs]
            k = k_ref[b, sl, hs]
            v = v_ref[b, sl, hs].astype(mm)
            st = s_scrs[h][b]
            b_cum = ee[0:L, hs]
            b_last = b_cum[L - 1:L, :]
            o = _dot_nt((q * jnp.exp(b_cum)).astype(mm), st.astype(mm))
            sc = _dot_nt(q.astype(mm), k.astype(mm)) * mk_ref[0]
            for l in range(nl):
                e = jnp.exp(ee[(1 + l) * L:(2 + l) * L, hs])
                sc = sc + _dot_nt((q * e).astype(mm), (k * e).astype(mm)) * mk_ref[1 + l]
            o = o + _dot(sc.astype(mm), v)
            kd = (k * jnp.exp(b_last - b_cum)).astype(mm)
            s_scrs[h][b] = st * jnp.exp(b_last) + _dot_tn(v, kd)
            og = _rms(o, go) * sg_ref[b, sl, hs]
            og_ref[b, sl, hs] = og.astype(og_ref.dtype)

    def body(i, carry):
        chunk(i // nchunk, i % nchunk)
        return carry

    lax.fori_loop(0, bb * nchunk, body, 0)

    @pl.when(ti == pl.num_programs(1) - 1)
    def _():
        def fin(b, carry):
            for h in range(nh):
                st_ref[b, h] = s_scrs[h][b].T
            return carry
        lax.fori_loop(0, bb, fin, 0)


def _gla(q, k, lf, v, sg, g_onorm, s0, bb, tb):
    B, T, hd = q.shape
    H = hd // HGRN_DK
    L = HGRN_CHUNK if T % HGRN_CHUNK == 0 else T
    tb = min(tb, T)
    bb = min(bb, B)
    mm = BF16 if L % 16 == 0 else F32
    me_np, mk_np, nl = _gla_constants(L)
    me = jnp.asarray(me_np, mm)
    mk = jnp.asarray(mk_np, F32)
    seq = pl.BlockSpec((bb, tb, hd), lambda b, t: (b, t, 0))
    st_spec = pl.BlockSpec((bb, H, HGRN_DK, HGRN_DK), lambda b, t: (b, 0, 0, 0))
    in_specs = [seq] * 5 + [
        pl.BlockSpec((1, HGRN_DK), lambda b, t: (0, 0)),
        pl.BlockSpec(me.shape, lambda b, t: (0, 0)),
        pl.BlockSpec(mk.shape, lambda b, t: (0, 0, 0)),
    ]
    args = [q, k, lf, v, sg, g_onorm.reshape(1, HGRN_DK), me, mk]
    if s0 is not None:
        in_specs.append(st_spec)
        args.append(s0)
    return pl.pallas_call(
        functools.partial(_gla_kernel, L, nl, H, s0 is not None, mm),
        grid=(B // bb, T // tb),
        in_specs=in_specs,
        out_specs=[seq, st_spec],
        out_shape=[jax.ShapeDtypeStruct((B, T, hd), BF16),
                   jax.ShapeDtypeStruct((B, H, HGRN_DK, HGRN_DK), F32)],
        scratch_shapes=[pltpu.VMEM((bb, HGRN_DK, HGRN_DK), F32) for _ in range(H)],
        compiler_params=_params(("parallel", "arbitrary")),
        name="gla",
    )(*args)


def _ffn_kernel(h_ref, og_ref, wo_ref, g_ref, wg_ref, wu_ref, wd_ref, out_ref,
                h1_scr, hn_scr, acc_scr):
    f = pl.program_id(1)

    @pl.when(f == 0)
    def _():
        h1 = h_ref[...] + _dot(og_ref[...], wo_ref[...])
        h1_scr[...] = h1
        hn_scr[...] = _rms(h1, g_ref[...]).astype(BF16)
        acc_scr[...] = jnp.zeros_like(acc_scr)

    hn = hn_scr[...]
    a = _dot(hn, wg_ref[...])
    u = _dot(hn, wu_ref[...])
    acc_scr[...] += _dot((a * _sigmoid(a) * u).astype(BF16), wd_ref[...])

    @pl.when(f == pl.num_programs(1) - 1)
    def _():
        out_ref[...] = h1_scr[...] + acc_scr[...]


def _pick_tf(dff, target):
    best = None
    for tf in range(LANE, dff + 1, LANE):
        if dff % tf == 0 and tf <= target:
            best = tf
    return best or dff


def _ffn(h, og, w_out_bf, g_ffn, wg_bf, wu_bf, wd_bf, tm, tf_target):
    n, d = h.shape
    dff = wg_bf.shape[1]
    tm = min(tm, n)
    tf = _pick_tf(dff, tf_target)
    row = lambda i, f: (i, 0)
    return pl.pallas_call(
        _ffn_kernel,
        grid=(n // tm, dff // tf),
        in_specs=[pl.BlockSpec((tm, d), row),
                  pl.BlockSpec((tm, d), row),
                  pl.BlockSpec((d, d), lambda i, f: (0, 0)),
                  pl.BlockSpec((1, d), lambda i, f: (0, 0)),
                  pl.BlockSpec((d, tf), lambda i, f: (0, f)),
                  pl.BlockSpec((d, tf), lambda i, f: (0, f)),
                  pl.BlockSpec((tf, d), lambda i, f: (f, 0))],
        out_specs=pl.BlockSpec((tm, d), row),
        out_shape=jax.ShapeDtypeStruct((n, d), F32),
        scratch_shapes=[pltpu.VMEM((tm, d), F32), pltpu.VMEM((tm, d), BF16),
                        pltpu.VMEM((tm, d), F32)],
        compiler_params=_params(("parallel", "arbitrary")),
        name="ffn",
    )(h, og, w_out_bf, g_ffn.reshape(1, d), wg_bf, wu_bf, wd_bf)


def _kvq_kernel(kv_lora, h_ref, cos_ref, sin_ref, gkin_ref, wkv_ref, wkvs_ref, gkv_ref,
                gmix_ref, wdq_ref, gq_ref, wqn_ref, wqr_ref, wqrs_ref, wuk_ref,
                c_ref, kr_ref, kcat_ref, qcat_ref):
    h = h_ref[...]
    cos = cos_ref[...]
    sin = sin_ref[...]
    nb = _rms(h, gkin_ref[...]).astype(BF16)
    kv = _dot(nb, wkv_ref[...])
    kvs = _dot(nb, wkvs_ref[...])
    c = _rms(kv[:, :kv_lora], gkv_ref[...])
    kr = kv[:, kv_lora:] * cos + kvs * sin
    c_ref[...] = c
    kr_ref[...] = kr[:, :QK_ROPE]
    kcat_ref[:, :kv_lora] = c.astype(BF16)
    kcat_ref[:, kv_lora:] = kr.astype(BF16)
    xb = _rms(h, gmix_ref[...]).astype(BF16)
    cq = _rms(_dot(xb, wdq_ref[...]), gq_ref[...]).astype(BF16)
    qn = _dot(cq, wqn_ref[...])
    qr = _dot(cq, wqr_ref[...])
    qrs = _dot(cq, wqrs_ref[...])
    for hh in range(MLA_HEADS):
        qn_h = qn[:, hh * QK_NOPE:(hh + 1) * QK_NOPE].astype(BF16)
        qa_h = _dot(qn_h, wuk_ref[hh]) * SM_SCALE
        sl = slice(hh * ROPE_PAD, (hh + 1) * ROPE_PAD)
        qr_h = (qr[:, sl] * cos + qrs[:, sl] * sin) * SM_SCALE
        qcat_ref[hh, :, :kv_lora] = qa_h.astype(qcat_ref.dtype)
        qcat_ref[hh, :, kv_lora:] = qr_h.astype(qcat_ref.dtype)


def _kvq(h2, cos_t, sin_t, p, q_dtype, tm):
    n, d = h2.shape
    tm = min(tm, cos_t.shape[0])
    kv_lora = p["g_kv"].shape[-1]
    kcw = kv_lora + ROPE_PAD
    nt = cos_t.shape[0] // tm
    const = lambda a: pl.BlockSpec(a.shape, lambda i: (0,) * a.ndim)
    row = lambda w: pl.BlockSpec((tm, w), lambda i: (i, 0))
    tab = pl.BlockSpec((tm, ROPE_PAD), lambda i: (i % nt, 0))
    weights = [p["g_kv_in"], p["w_kv"], p["w_kvs"], p["g_kv"], p["g_mix_b"], p["w_dq"],
               p["g_q"], p["w_qn"], p["w_qr"], p["w_qrs"], p["w_uk"]]
    return pl.pallas_call(
        functools.partial(_kvq_kernel, kv_lora),
        grid=(n // tm,),
        in_specs=[row(d), tab, tab] + [const(w) for w in weights],
        out_specs=[row(kv_lora), row(QK_ROPE), row(kcw),
                   pl.BlockSpec((MLA_HEADS, tm, kcw), lambda i: (0, i, 0))],
        out_shape=[jax.ShapeDtypeStruct((n, kv_lora), F32),
                   jax.ShapeDtypeStruct((n, QK_ROPE), F32),
                   jax.ShapeDtypeStruct((n, kcw), BF16),
                   jax.ShapeDtypeStruct((MLA_HEADS, n, kcw), q_dtype)],
        compiler_params=_params(("parallel",)),
        name="kvq",
    )(h2, cos_t, sin_t, *weights)


def _attn_prompt_kernel(tq, tk, kv_lora, hg, qi_ref, kj_ref, first_ref, last_ref, diag_ref,
                        q_ref, k_ref, o_ref, *scr):
    p_id = pl.program_id(1)
    qi = qi_ref[p_id]
    kj = kj_ref[p_id]
    gr = hg * tq
    ngr = MLA_HEADS // hg
    m_scrs, l_scrs, acc_scrs = scr[:ngr], scr[ngr:2 * ngr], scr[2 * ngr:]

    @pl.when(first_ref[p_id] == 1)
    def _():
        for g in range(ngr):
            m_scrs[g][...] = jnp.full_like(m_scrs[g], NEG_INF)
            l_scrs[g][...] = jnp.zeros_like(l_scrs[g])
            acc_scrs[g][...] = jnp.zeros_like(acc_scrs[g])

    def step(masked):
        k = k_ref[...]
        kv = k[:, :kv_lora]
        if masked:
            q_pos = qi * tq + lax.broadcasted_iota(jnp.int32, (gr, tk), 0) % tq
            k_pos = kj * tk + lax.broadcasted_iota(jnp.int32, (gr, tk), 1)
            keep = k_pos <= q_pos
        for g in range(ngr):
            q = q_ref[g * hg:(g + 1) * hg].reshape(gr, q_ref.shape[-1])
            s = _dot_nt(q, k)
            if masked:
                s = jnp.where(keep, s, NEG_INF)
            m_prev = m_scrs[g][...]
            m_new = jnp.maximum(m_prev,
                                jnp.max(_lane_fold(s, jnp.maximum), axis=-1, keepdims=True))
            alpha = jnp.exp(m_prev - m_new)
            pr = jnp.exp(s - m_new)
            l_scrs[g][...] = (alpha * l_scrs[g][...]
                              + jnp.sum(_lane_fold(pr, jnp.add), axis=-1, keepdims=True))
            acc_scrs[g][...] = alpha * acc_scrs[g][...] + _dot(pr.astype(BF16), kv)
            m_scrs[g][...] = m_new

    @pl.when(diag_ref[p_id] == 1)
    def _():
        step(True)

    @pl.when(diag_ref[p_id] == 0)
    def _():
        step(False)

    @pl.when(last_ref[p_id] == 1)
    def _():
        for g in range(ngr):
            o = acc_scrs[g][...] / l_scrs[g][...]
            o_ref[g * hg:(g + 1) * hg] = o.reshape(hg, tq, kv_lora).astype(o_ref.dtype)


def _attn_prompt(qcat, kcat, B, T, kv_lora, tq, tk, hg):
    tq = min(tq, T)
    tk = min(tk, T)
    nq, nk = T // tq, T // tk
    qi, kj, first, last, diag = [], [], [], [], []
    for i in range(nq):
        need = -(-((i + 1) * tq) // tk)
        for j in range(need):
            qi.append(i); kj.append(j); first.append(int(j == 0)); last.append(int(j == need - 1))
            diag.append(int((j + 1) * tk - 1 > i * tq))
    tabs = [jnp.asarray(np.array(a, np.int32)) for a in (qi, kj, first, last, diag)]
    kcw = kcat.shape[-1]
    gr, ngr = hg * tq, MLA_HEADS // hg
    qmap = lambda b, p, qi, kj, fi, la, di: (0, b * nq + qi[p], 0)
    grid_spec = pltpu.PrefetchScalarGridSpec(
        num_scalar_prefetch=5,
        grid=(B, len(qi)),
        in_specs=[pl.BlockSpec((MLA_HEADS, tq, kcw), qmap),
                  pl.BlockSpec((tk, kcw), lambda b, p, qi, kj, fi, la, di: (b * nk + kj[p], 0))],
        out_specs=pl.BlockSpec((MLA_HEADS, tq, kv_lora), qmap),
        scratch_shapes=[pltpu.VMEM((gr, 1), F32) for _ in range(2 * ngr)]
                       + [pltpu.VMEM((gr, kv_lora), F32) for _ in range(ngr)],
    )
    return pl.pallas_call(
        functools.partial(_attn_prompt_kernel, tq, tk, kv_lora, hg),
        grid_spec=grid_spec,
        out_shape=jax.ShapeDtypeStruct((MLA_HEADS, B * T, kv_lora), BF16),
        compiler_params=_params(("parallel", "arbitrary")),
        name="attn_prompt",
    )(*tabs, qcat, kcat)


def _attn_sample_kernel(pps, gp, kv_lora, pt_ref, q_ref, cn_ref, krn_ref, *refs):
    c_refs = refs[:pps]
    kr_refs = refs[pps:2 * pps]
    o_ref = refs[2 * pps]
    m_scr, l_scr, acc_scr = refs[2 * pps + 1:]
    g = pl.program_id(1)
    t_new = q_ref.shape[1]
    rows = MLA_HEADS * t_new
    q = q_ref[...].reshape(rows, q_ref.shape[-1])
    qa = q[:, :kv_lora]
    qr = q[:, kv_lora:kv_lora + QK_ROPE]

    @pl.when(g == 0)
    def _():
        cn = cn_ref[...]
        s = _dot_nt(qa, cn) + _dot_nt(qr, krn_ref[...])
        q_t = lax.broadcasted_iota(jnp.int32, s.shape, 0) % t_new
        k_t = lax.broadcasted_iota(jnp.int32, s.shape, 1)
        s = jnp.where(k_t <= q_t, s, NEG_INF)
        m = jnp.max(s, axis=-1, keepdims=True)
        pr = jnp.exp(s - m)
        m_scr[...] = m
        l_scr[...] = jnp.sum(pr, axis=-1, keepdims=True)
        acc_scr[...] = _dot(pr, cn)

    qa_b = qa.astype(BF16)
    qr_b = qr.astype(BF16)
    parts = []
    for grp in range(pps // gp):
        pages = range(grp * gp, (grp + 1) * gp)
        kc = jnp.concatenate([c_refs[j][...].astype(BF16) for j in pages], axis=0)
        kk = jnp.concatenate([kr_refs[j][...].astype(BF16) for j in pages], axis=1)
        s = _dot_nt(qa_b, kc) + _dot(qr_b, kk)
        m_g = jnp.max(s, axis=-1, keepdims=True)
        pr = jnp.exp(s - m_g)
        parts.append((m_g, jnp.sum(pr, axis=-1, keepdims=True), _dot(pr.astype(BF16), kc)))
    m_prev = m_scr[...]
    m_new = m_prev
    for m_g, _, _ in parts:
        m_new = jnp.maximum(m_new, m_g)
    alpha = jnp.exp(m_prev - m_new)
    l_new = alpha * l_scr[...]
    acc = alpha * acc_scr[...]
    for m_g, l_g, o_g in parts:
        w = jnp.exp(m_g - m_new)
        l_new = l_new + w * l_g
        acc = acc + w * o_g
    m_scr[...] = m_new
    l_scr[...] = l_new
    acc_scr[...] = acc

    @pl.when(g == pl.num_programs(1) - 1)
    def _():
        o = acc_scr[...] / l_scr[...]
        o_ref[...] = o.reshape(MLA_HEADS, t_new, kv_lora)


def _attn_sample(qcat, c_new, kr_new, cache_ckv, cache_krope, page_table, pps, gp):
    B, n_pages = page_table.shape
    t_new = c_new.shape[0] // B
    kv_lora = c_new.shape[-1]
    kcw = qcat.shape[-1]
    pps = min(pps, n_pages)
    while n_pages % pps:
        pps -= 1
    gp = min(gp, pps)
    while pps % gp:
        gp -= 1
    rows = MLA_HEADS * t_new
    pt_flat = page_table.reshape(-1)
    krope_t = jnp.swapaxes(cache_krope, 1, 2)

    def page_map(j):
        return lambda b, g, pt: (pt[b * n_pages + g * pps + j], 0, 0)

    grid_spec = pltpu.PrefetchScalarGridSpec(
        num_scalar_prefetch=1,
        grid=(B, n_pages // pps),
        in_specs=[pl.BlockSpec((MLA_HEADS, t_new, kcw), lambda b, g, pt: (0, b, 0)),
                  pl.BlockSpec((t_new, kv_lora), lambda b, g, pt: (b, 0)),
                  pl.BlockSpec((t_new, QK_ROPE), lambda b, g, pt: (b, 0))]
                 + [pl.BlockSpec((None, PAGE_SIZE, kv_lora), page_map(j)) for j in range(pps)]
                 + [pl.BlockSpec((None, QK_ROPE, PAGE_SIZE), page_map(j)) for j in range(pps)],
        out_specs=pl.BlockSpec((MLA_HEADS, t_new, kv_lora), lambda b, g, pt: (0, b, 0)),
        scratch_shapes=[pltpu.VMEM((rows, 1), F32), pltpu.VMEM((rows, 1), F32),
                        pltpu.VMEM((rows, kv_lora), F32)],
    )
    return pl.pallas_call(
        functools.partial(_attn_sample_kernel, pps, gp, kv_lora),
        grid_spec=grid_spec,
        out_shape=jax.ShapeDtypeStruct((MLA_HEADS, B * t_new, kv_lora), F32),
        compiler_params=_params(("parallel", "arbitrary")),
        name="attn_sample",
    )(pt_flat, qcat, c_new, kr_new, *([cache_ckv] * pps), *([krope_t] * pps))


def _route_kernel(n_exp, ne8, h_ref, o_ref, wuv_ref, wo_ref, g_ref, wr_ref, tri_ref,
                  h3_ref, hn_ref, comb_ref, rtm_ref, rem_ref, cnt_ref):
    heads = [_dot(o_ref[hh].astype(BF16), wuv_ref[hh]).astype(BF16) for hh in range(MLA_HEADS)]
    h3 = h_ref[...] + _dot(jnp.concatenate(heads, axis=-1), wo_ref[...])
    h3_ref[...] = h3
    hn = _rms(h3, g_ref[...])
    hn_ref[...] = hn.astype(BF16)
    lg = jnp.dot(hn, wr_ref[...], preferred_element_type=F32, precision=lax.Precision.HIGHEST)
    lane = lax.broadcasted_iota(jnp.int32, lg.shape, 1)
    lg = jnp.where(lane < n_exp, lg, -jnp.inf)
    m1 = jnp.max(lg, axis=-1, keepdims=True)
    i1 = jnp.min(jnp.where(lg == m1, lane, LANE), axis=-1, keepdims=True)
    lg2 = jnp.where(lane == i1, -jnp.inf, lg)
    m2 = jnp.max(lg2, axis=-1, keepdims=True)
    i2 = jnp.min(jnp.where(lg2 == m2, lane, LANE), axis=-1, keepdims=True)
    t = jnp.exp(m2 - m1)
    g1 = 1.0 / (1.0 + t)
    comb_ref[...] = jnp.where(lane == i1, g1, 0.0) + jnp.where(lane == i2, t * g1, 0.0)
    routed = jnp.where(lane == i1, 1.0, 0.0) + jnp.where(lane == i2, 1.0, 0.0)
    rank = _dot(tri_ref[...], routed.astype(BF16))
    rtm = jnp.where(routed > 0.0, rank, -1.0)
    rtm_ref[...] = rtm
    rem_ref[...] = rtm.T[:ne8]
    cnt_ref[...] = jnp.max(rtm, axis=0, keepdims=True) + 1.0


def _route(h2, o_lat, p, sb):
    n, d = h2.shape
    n_exp = p["w_e_gate"].shape[0]
    ne8 = -(-n_exp // SUBLANE) * SUBLANE
    kv_lora = o_lat.shape[-1]
    nb = n // sb
    tri = jnp.asarray(np.tril(np.ones((sb, sb), np.float32), -1), BF16)
    c2 = lambda i: (0, 0)
    row = lambda w: pl.BlockSpec((sb, w), lambda i: (i, 0))
    return pl.pallas_call(
        functools.partial(_route_kernel, n_exp, ne8),
        grid=(nb,),
        in_specs=[row(d),
                  pl.BlockSpec((MLA_HEADS, sb, kv_lora), lambda i: (0, i, 0)),
                  pl.BlockSpec(p["w_uv"].shape, lambda i: (0, 0, 0)),
                  pl.BlockSpec((d, d), c2),
                  pl.BlockSpec((1, d), c2),
                  pl.BlockSpec((d, LANE), c2),
                  pl.BlockSpec((sb, sb), c2)],
        out_specs=[row(d), row(d), row(LANE), row(LANE),
                   pl.BlockSpec((None, ne8, sb), lambda i: (i, 0, 0)),
                   pl.BlockSpec((None, 1, LANE), lambda i: (i, 0, 0))],
        out_shape=[jax.ShapeDtypeStruct((n, d), F32),
                   jax.ShapeDtypeStruct((n, d), BF16),
                   jax.ShapeDtypeStruct((n, LANE), F32),
                   jax.ShapeDtypeStruct((n, LANE), F32),
                   jax.ShapeDtypeStruct((nb, ne8, sb), F32),
                   jax.ShapeDtypeStruct((nb, 1, LANE), F32)],
        compiler_params=_params(("parallel",)),
        name="route",
    )(h2, o_lat, p["w_uv"], p["w_out_b"], p["g_ffn1"], p["w_router"], tri)


def _moe_kernel(rt, tmax, cnt_ref, hn_ref, rem_ref, rtm_ref, comb_ref, h3_ref, gfin_ref,
                wg_ref, wu_ref, wd_ref, out_ref, xg_scr, y_scr):
    j = pl.program_id(0)
    e = pl.program_id(1)
    f = pl.program_id(2)
    ne = pl.num_programs(1)
    nf = pl.num_programs(2)
    sb = hn_ref.shape[0]
    n_t = (cnt_ref[j * ne + e] + rt - 1) // rt

    @pl.when((e == 0) & (f == 0))
    def _():
        out_ref[...] = h3_ref[...]

    @pl.when(f == 0)
    def _():
        rank_row = rem_ref[pl.ds(e, 1), :]
        for t in range(tmax):
            @pl.when(t < n_t)
            def _():
                r = (t * rt + lax.broadcasted_iota(jnp.int32, (rt, sb), 0)).astype(F32)
                onehot = jnp.where(rank_row == r, 1.0, 0.0).astype(BF16)
                xg_scr[t] = _dot(onehot, hn_ref[...]).astype(BF16)
                y_scr[t] = jnp.zeros((rt, y_scr.shape[-1]), F32)

    for t in range(tmax):
        @pl.when(t < n_t)
        def _():
            x = xg_scr[t]
            a = _dot(x, wg_ref[...])
            u = _dot(x, wu_ref[...])
            y_scr[t] += _dot((a * _sigmoid(a) * u).astype(BF16), wd_ref[...])

    @pl.when(f == nf - 1)
    def _():
        sel = lax.broadcasted_iota(jnp.int32, (sb, LANE), 1) == e
        rank_col = jnp.sum(jnp.where(sel, rtm_ref[...], 0.0), axis=-1, keepdims=True)
        gate_col = jnp.sum(jnp.where(sel, comb_ref[...], 0.0), axis=-1, keepdims=True)
        for t in range(tmax):
            @pl.when(t < n_t)
            def _():
                r = (t * rt + lax.broadcasted_iota(jnp.int32, (sb, rt), 1)).astype(F32)
                onehot_t = jnp.where(rank_col == r, 1.0, 0.0).astype(BF16)
                out_ref[...] += gate_col * _dot(onehot_t, y_scr[t].astype(BF16))

    @pl.when((e == ne - 1) & (f == nf - 1))
    def _():
        out_ref[...] = _rms(out_ref[...], gfin_ref[...])


def _moe(h3, hn, comb, rtm, rem, cnt, p, sb, rt, tf_target):
    n, d = h3.shape
    n_exp, _, dff = p["w_e_gate"].shape
    tf = _pick_tf(dff, tf_target)
    rt = min(rt, sb)
    tmax = -(-sb // rt)
    ne8 = rem.shape[1]
    once = dict(pipeline_mode=pl.Buffered(1))
    grid_spec = pltpu.PrefetchScalarGridSpec(
        num_scalar_prefetch=1,
        grid=(n // sb, n_exp, dff // tf),
        in_specs=[pl.BlockSpec((sb, d), lambda j, e, f, c: (j, 0), **once),
                  pl.BlockSpec((None, ne8, sb), lambda j, e, f, c: (j, 0, 0), **once),
                  pl.BlockSpec((sb, LANE), lambda j, e, f, c: (j, 0), **once),
                  pl.BlockSpec((sb, LANE), lambda j, e, f, c: (j, 0), **once),
                  pl.BlockSpec((sb, d), lambda j, e, f, c: (j, 0), **once),
                  pl.BlockSpec((1, d), lambda j, e, f, c: (0, 0)),
                  pl.BlockSpec((None, d, tf), lambda j, e, f, c: (e, 0, f)),
                  pl.BlockSpec((None, d, tf), lambda j, e, f, c: (e, 0, f)),
                  pl.BlockSpec((None, tf, d), lambda j, e, f, c: (e, f, 0))],
        out_specs=pl.BlockSpec((sb, d), lambda j, e, f, c: (j, 0)),
        scratch_shapes=[pltpu.VMEM((tmax, rt, d), BF16), pltpu.VMEM((tmax, rt, d), F32)],
    )
    return pl.pallas_call(
        functools.partial(_moe_kernel, rt, tmax),
        grid_spec=grid_spec,
        out_shape=jax.ShapeDtypeStruct((n, d), F32),
        compiler_params=_params(("parallel", "arbitrary", "arbitrary")),
        name="moe",
    )(cnt, hn, rem, rtm, comb, h3, p["g_final"], p["w_e_gate"], p["w_e_up"], p["w_e_down"])


def _rope_tables(pos):
    half = QK_ROPE // 2
    inv = ROPE_THETA ** (-jnp.arange(half, dtype=F32) / half)
    ang = pos.astype(F32)[:, None] * inv[None, :]
    cos, sin = jnp.cos(ang), jnp.sin(ang)
    z = jnp.zeros((pos.shape[0], ROPE_PAD - QK_ROPE), F32)
    return (jnp.concatenate([cos, cos, z], axis=-1), jnp.concatenate([-sin, sin, z], axis=-1))


def _swap_halves(w):
    half = w.shape[-1] // 2
    return jnp.concatenate([w[..., half:], w[..., :half]], axis=-1)


def _pad_rope(w):
    return jnp.pad(w, [(0, 0)] * (w.ndim - 1) + [(0, ROPE_PAD - QK_ROPE)])


def _prepare(g_mix_a, w_in_a, gamma_lb, g_onorm_a, w_out_a, g_kv_in, w_dkv, g_kv, w_ukv,
             g_mix_b, w_dq, g_q, w_uq, w_out_b, g_ffn, w_ff_gate, w_ff_up, w_ff_down,
             w_router, w_e_gate, w_e_up, w_e_down, g_final):
    kv_lora = g_kv.shape[-1]
    q_lora = g_q.shape[-1]
    row = lambda g: g.reshape(1, -1)
    wk_rope = w_dkv[:, kv_lora:]
    w_uq3 = w_uq[0].reshape(q_lora, MLA_HEADS, QK_NOPE + QK_ROPE)
    wq_rope = w_uq3[..., QK_NOPE:]
    w_ukv3 = w_ukv.reshape(kv_lora, MLA_HEADS, QK_NOPE + V_DIM)
    n_exp = w_router.shape[-1]
    return {
        "g_mix_a": g_mix_a[0], "w_in": w_in_a[0].astype(BF16), "gamma_lb": gamma_lb,
        "g_onorm": g_onorm_a[0], "w_out_a": w_out_a[0].astype(BF16),
        "g_ffn0": g_ffn[0], "w_ff_gate": w_ff_gate[0].astype(BF16),
        "w_ff_up": w_ff_up[0].astype(BF16), "w_ff_down": w_ff_down[0].astype(BF16),
        "g_kv_in": row(g_kv_in), "g_kv": row(g_kv), "g_mix_b": row(g_mix_b[0]), "g_q": row(g_q[0]),
        "w_kv": jnp.concatenate([w_dkv[:, :kv_lora], _pad_rope(wk_rope)], axis=-1).astype(BF16),
        "w_kvs": _pad_rope(_swap_halves(wk_rope)).astype(BF16),
        "w_dq": w_dq[0].astype(BF16),
        "w_qn": w_uq3[..., :QK_NOPE].reshape(q_lora, -1).astype(BF16),
        "w_qr": _pad_rope(wq_rope).reshape(q_lora, -1).astype(BF16),
        "w_qrs": _pad_rope(_swap_halves(wq_rope)).reshape(q_lora, -1).astype(BF16),
        "w_uk": jnp.transpose(w_ukv3[..., :QK_NOPE], (1, 2, 0)).astype(BF16),
        "w_uv": jnp.transpose(w_ukv3[..., QK_NOPE:], (1, 0, 2)).astype(BF16),
        "w_out_b": w_out_b[0].astype(BF16),
        "g_ffn1": row(g_ffn[1]), "g_final": row(g_final),
        "w_router": jnp.pad(w_router[0], ((0, 0), (0, LANE - n_exp))),
        "w_e_gate": w_e_gate[0].astype(BF16), "w_e_up": w_e_up[0].astype(BF16),
        "w_e_down": w_e_down[0].astype(BF16),
    }


def _trunk(x, pos, s0, p, attend, q_dtype, gla_bb, tm):
    B, T, d = x.shape
    n = B * T
    x2 = x.reshape(n, d)
    q, k, lf, v, sg = _hgrn_proj(x2, p["g_mix_a"], p["gamma_lb"], p["w_in"], 0, tm)
    r3 = lambda a: a.reshape(B, T, d)
    og, s_t = _gla(r3(q), r3(k), r3(lf), r3(v), r3(sg), p["g_onorm"], s0, gla_bb, 512)
    h2 = _ffn(x2, og.reshape(n, d), p["w_out_a"], p["g_ffn0"], p["w_ff_gate"], p["w_ff_up"],
              p["w_ff_down"], tm, 1408)
    cos_t, sin_t = _rope_tables(pos)
    c, kr, kcat, qcat = _kvq(h2, cos_t, sin_t, p, q_dtype, tm)
    o_lat = attend(qcat, kcat, c, kr)
    sb = min(1024, n)
    h3, hn, comb, rtm, rem, cnt = _route(h2, o_lat, p, sb)
    n_exp = p["w_e_gate"].shape[0]
    cnt_i = cnt[:, 0, :n_exp].astype(jnp.int32).reshape(-1)
    y = _moe(h3, hn, comb, rtm, rem, cnt_i, p, sb, 384, 1408)
    kv_lora = c.shape[-1]
    return (y.reshape(B, T, d), c.reshape(B, T, kv_lora), kr.reshape(B, T, QK_ROPE), s_t[None])


def kernel(x_prompt, x_sample, cache_ckv, cache_krope, state_hgrn, page_table, g_mix_a, w_in_a,
           gamma_lb, g_onorm_a, w_out_a, g_kv_in, w_dkv, g_kv, w_ukv, g_mix_b, w_dq, g_q, w_uq,
           w_out_b, g_ffn, w_ff_gate, w_ff_up, w_ff_down, w_router, w_e_gate, w_e_up, w_e_down,
           g_final):
    p = _prepare(g_mix_a, w_in_a, gamma_lb, g_onorm_a, w_out_a, g_kv_in, w_dkv, g_kv, w_ukv,
                 g_mix_b, w_dq, g_q, w_uq, w_out_b, g_ffn, w_ff_gate, w_ff_up, w_ff_down,
                 w_router, w_e_gate, w_e_up, w_e_down, g_final)
    kv_lora = g_kv.shape[-1]
    bp, tp, _ = x_prompt.shape
    bs, ts, _ = x_sample.shape

    def attend_p(qcat, kcat, c, kr):
        return _attn_prompt(qcat, kcat, bp, tp, kv_lora, 256, 512, 4)

    y_p, c_p, kr_p, s_p = _trunk(x_prompt, jnp.arange(tp, dtype=F32), None, p, attend_p,
                                 BF16, 1, 512)

    past = page_table.shape[1] * PAGE_SIZE
    pos_s = jnp.tile(past + jnp.arange(ts, dtype=F32), bs)

    def attend_s(qcat, kcat, c, kr):
        return _attn_sample(qcat, c, kr, cache_ckv, cache_krope, page_table, 16, 4)

    y_s, c_s, kr_s, s_s = _trunk(x_sample, pos_s, state_hgrn[0], p, attend_s, F32, 8, 512)
    return (y_p, y_s, c_p, kr_p, c_s, kr_s, s_p.astype(state_hgrn.dtype),
            s_s.astype(state_hgrn.dtype))
```

```python
import functools

import numpy as np
import jax
import jax.numpy as jnp
from jax import lax
from jax.experimental import pallas as pl
from jax.experimental.pallas import tpu as pltpu

F32 = jnp.float32
BF16 = jnp.bfloat16

RMS_EPS = 1e-6
ROPE_THETA = 10000.0
HGRN_HEADS = 8
HGRN_DK = 128
HGRN_CHUNK = 64
MLA_HEADS = 8
QK_NOPE = 128
QK_ROPE = 64
V_DIM = 128
PAGE_SIZE = 128
TOP_K = 2
SM_SCALE = (QK_NOPE + QK_ROPE) ** -0.5
NEG_INF = -1e30
LANE = 128
SUBLANE = 8
ROPE_PAD = LANE
VMEM_LIMIT = 56 * 1024 * 1024


def _params(sem):
    return pltpu.CompilerParams(dimension_semantics=sem, vmem_limit_bytes=VMEM_LIMIT)


def _dot(a, b):
    return jnp.dot(a, b, preferred_element_type=F32)


def _dot_nt(a, b):
    return lax.dot_general(a, b, (((1,), (1,)), ((), ())), preferred_element_type=F32)


def _dot_tn(a, b):
    return lax.dot_general(a, b, (((0,), (0,)), ((), ())), preferred_element_type=F32)


def _rms(x, g):
    return x * lax.rsqrt(jnp.mean(x * x, axis=-1, keepdims=True) + RMS_EPS) * g


def _sigmoid(x):
    return 1.0 / (1.0 + jnp.exp(-x))


def _lane_fold(x, op):
    w = x.shape[-1]
    if w % LANE or w == LANE:
        return x
    acc = x[:, :LANE]
    for j in range(1, w // LANE):
        acc = op(acc, x[:, j * LANE:(j + 1) * LANE])
    return acc


def _hgrn_proj_kernel(layer, x_ref, g_ref, gam_ref, w_ref, q_ref, k_ref, lf_ref, i_ref, sg_ref):
    d = x_ref.shape[-1]
    xb = _rms(x_ref[...], g_ref[...]).astype(BF16)

    def proj(j):
        return _dot(xb, w_ref[:, j * d:(j + 1) * d])

    gam = gam_ref[...]
    e = jnp.exp(gam - jnp.max(gam, axis=0, keepdims=True))
    lb = jnp.sum(e[:layer + 1], axis=0, keepdims=True) / jnp.sum(e, axis=0, keepdims=True)

    q = proj(0)
    q_ref[...] = (q * _sigmoid(q)).astype(q_ref.dtype)
    z = proj(1)
    t = jnp.exp(-jnp.abs(z))
    r = 1.0 / (1.0 + t)
    pos = z >= 0
    sig_z = jnp.where(pos, r, t * r)
    sig_mz = jnp.where(pos, t * r, r)
    k_ref[...] = ((1.0 - lb) * sig_mz).astype(k_ref.dtype)
    lf_ref[...] = jnp.log(lb + (1.0 - lb) * sig_z)
    i_ref[...] = proj(2).astype(i_ref.dtype)
    g = proj(3)
    sg_ref[...] = (g * _sigmoid(g)).astype(sg_ref.dtype)


def _hgrn_proj(x2, g_mix, gamma_lb, w_in_bf, layer, tm, act_dtype):
    n, d = x2.shape
    tm = min(tm, n)
    row = pl.BlockSpec((tm, d), lambda i: (i, 0))
    out = jax.ShapeDtypeStruct((n, d), act_dtype)
    return pl.pallas_call(
        functools.partial(_hgrn_proj_kernel, layer),
        grid=(n // tm,),
        in_specs=[row,
                  pl.BlockSpec((1, d), lambda i: (0, 0)),
                  pl.BlockSpec(gamma_lb.shape, lambda i: (0, 0)),
                  pl.BlockSpec(w_in_bf.shape, lambda i: (0, 0))],
        out_specs=[row] * 5,
        out_shape=[out, out, jax.ShapeDtypeStruct((n, d), F32), out, out],
        compiler_params=_params(("parallel",)),
        name="hgrn_proj",
    )(x2, g_mix.reshape(1, d), gamma_lb, w_in_bf)


def _gla_constants(L):
    nl = int(np.log2(L))
    assert 2 ** nl == L
    t = np.arange(L)
    me = np.zeros((1 + nl, L, L), np.float32)
    me[0] = (t[None, :] <= t[:, None])
    masks = np.zeros((1 + nl, L, L), np.float32)
    masks[0] = np.eye(L)
    for l in range(nl):
        m = L >> (l + 1)
        blk = t // (2 * m)
        r = blk * 2 * m + m
        upper = t >= r
        j = t[None, :]
        up_rows = (j >= r[:, None]) & (j <= t[:, None])
        lo_rows = (j > t[:, None]) & (j <= r[:, None] - 1)
        me[1 + l] = np.where(upper[:, None], up_rows, lo_rows)
        masks[1 + l] = (upper[:, None] & ~upper[None, :] & (blk[:, None] == blk[None, :]))
    return me.reshape((1 + nl) * L, L), masks, nl


def _gla_kernel(L, nl, nh, has_s0, mm, *refs):
    s_scrs = refs[-nh:]
    refs = refs[:-nh]
    if has_s0:
        (q_ref, k_ref, lf_ref, v_ref, sg_ref, go_ref, me_ref, mk_ref, s0_ref,
         og_ref, st_ref) = refs
    else:
        (q_ref, k_ref, lf_ref, v_ref, sg_ref, go_ref, me_ref, mk_ref,
         og_ref, st_ref) = refs
        s0_ref = None
    bb, tb, _ = q_ref.shape
    nchunk = tb // L
    ti = pl.program_id(1)
    dk = HGRN_DK

    @pl.when(ti == 0)
    def _():
        if has_s0:
            def init(b, carry):
                for h in range(nh):
                    s_scrs[h][b] = s0_ref[b, h].T
                return carry
            lax.fori_loop(0, bb, init, 0)
        else:
            for h in range(nh):
                s_scrs[h][...] = jnp.zeros_like(s_scrs[h])

    me = me_ref[...]
    go = go_ref[...]

    def chunk(b, c):
        sl = pl.ds(pl.multiple_of(c * L, L), L)
        g = lf_ref[b, sl, :]
        if mm == BF16:
            g1 = g.astype(BF16)
            g2 = (g - g1.astype(F32)).astype(BF16)
            ee = _dot(me, g1) + _dot(me, g2)
        else:
            ee = _dot(me, g)
        for h in range(nh):
            hs = slice(h * dk, (h + 1) * dk)
            q = q_ref[b, sl, hs].astype(F32)
            k = k_ref[b, sl, hs].astype(F32)
            v = v_ref[b, sl, hs].astype(mm)
            st = s_scrs[h][b]
            b_cum = ee[0:L, hs]
            b_last = b_cum[L - 1:L, :]
            o = _dot_nt((q * jnp.exp(b_cum)).astype(mm), st.astype(mm))
            sc = _dot_nt(q.astype(mm), k.astype(mm)) * mk_ref[0]
            for l in range(nl):
                e = jnp.exp(ee[(1 + l) * L:(2 + l) * L, hs])
                sc = sc + _dot_nt((q * e).astype(mm), (k * e).astype(mm)) * mk_ref[1 + l]
            o = o + _dot(sc.astype(mm), v)
            kd = (k * jnp.exp(b_last - b_cum)).astype(mm)
            s_scrs[h][b] = st * jnp.exp(b_last) + _dot_tn(v, kd)
            og = _rms(o, go) * sg_ref[b, sl, hs].astype(F32)
            og_ref[b, sl, hs] = og.astype(og_ref.dtype)

    def body(i, carry):
        chunk(i // nchunk, i % nchunk)
        return carry

    lax.fori_loop(0, bb * nchunk, body, 0)

    @pl.when(ti == pl.num_programs(1) - 1)
    def _():
        def fin(b, carry):
            for h in range(nh):
                st_ref[b, h] = s_scrs[h][b].T
            return carry
        lax.fori_loop(0, bb, fin, 0)


def _gla(q, k, lf, v, sg, g_onorm, s0, bb, tb):
    B, T, hd = q.shape
    H = hd // HGRN_DK
    L = HGRN_CHUNK if T % HGRN_CHUNK == 0 else T
    tb = min(tb, T)
    bb = min(bb, B)
    mm = BF16 if L % 16 == 0 else F32
    me_np, mk_np, nl = _gla_constants(L)
    me = jnp.asarray(me_np, mm)
    mk = jnp.asarray(mk_np, F32)
    seq = pl.BlockSpec((bb, tb, hd), lambda b, t: (b, t, 0))
    st_spec = pl.BlockSpec((bb, H, HGRN_DK, HGRN_DK), lambda b, t: (b, 0, 0, 0))
    in_specs = [seq] * 5 + [
        pl.BlockSpec((1, HGRN_DK), lambda b, t: (0, 0)),
        pl.BlockSpec(me.shape, lambda b, t: (0, 0)),
        pl.BlockSpec(mk.shape, lambda b, t: (0, 0, 0)),
    ]
    args = [q, k, lf, v, sg, g_onorm.reshape(1, HGRN_DK), me, mk]
    if s0 is not None:
        in_specs.append(st_spec)
        args.append(s0)
    return pl.pallas_call(
        functools.partial(_gla_kernel, L, nl, H, s0 is not None, mm),
        grid=(B // bb, T // tb),
        in_specs=in_specs,
        out_specs=[seq, st_spec],
        out_shape=[jax.ShapeDtypeStruct((B, T, hd), BF16),
                   jax.ShapeDtypeStruct((B, H, HGRN_DK, HGRN_DK), F32)],
        scratch_shapes=[pltpu.VMEM((bb, HGRN_DK, HGRN_DK), F32) for _ in range(H)],
        compiler_params=_params(("parallel", "arbitrary")),
        name="gla",
    )(*args)


def _ffn_kernel(h_ref, og_ref, wo_ref, g_ref, wg_ref, wu_ref, wd_ref, out_ref,
                h1_scr, hn_scr, acc_scr):
    f = pl.program_id(1)

    @pl.when(f == 0)
    def _():
        h1 = h_ref[...] + _dot(og_ref[...], wo_ref[...])
        h1_scr[...] = h1
        hn_scr[...] = _rms(h1, g_ref[...]).astype(BF16)
        acc_scr[...] = jnp.zeros_like(acc_scr)

    hn = hn_scr[...]
    a = _dot(hn, wg_ref[...])
    u = _dot(hn, wu_ref[...])
    acc_scr[...] += _dot((a * _sigmoid(a) * u).astype(BF16), wd_ref[...])

    @pl.when(f == pl.num_programs(1) - 1)
    def _():
        out_ref[...] = h1_scr[...] + acc_scr[...]


def _pick_tf(dff, target):
    best = None
    for tf in range(LANE, dff + 1, LANE):
        if dff % tf == 0 and tf <= target:
            best = tf
    return best or dff


def _ffn(h, og, w_out_bf, g_ffn, wg_bf, wu_bf, wd_bf, tm, tf_target):
    n, d = h.shape
    dff = wg_bf.shape[1]
    tm = min(tm, n)
    tf = _pick_tf(dff, tf_target)
    row = lambda i, f: (i, 0)
    return pl.pallas_call(
        _ffn_kernel,
        grid=(n // tm, dff // tf),
        in_specs=[pl.BlockSpec((tm, d), row),
                  pl.BlockSpec((tm, d), row),
                  pl.BlockSpec((d, d), lambda i, f: (0, 0)),
                  pl.BlockSpec((1, d), lambda i, f: (0, 0)),
                  pl.BlockSpec((d, tf), lambda i, f: (0, f)),
                  pl.BlockSpec((d, tf), lambda i, f: (0, f)),
                  pl.BlockSpec((tf, d), lambda i, f: (f, 0))],
        out_specs=pl.BlockSpec((tm, d), row),
        out_shape=jax.ShapeDtypeStruct((n, d), F32),
        scratch_shapes=[pltpu.VMEM((tm, d), F32), pltpu.VMEM((tm, d), BF16),
                        pltpu.VMEM((tm, d), F32)],
        compiler_params=_params(("parallel", "arbitrary")),
        name="ffn",
    )(h, og, w_out_bf, g_ffn.reshape(1, d), wg_bf, wu_bf, wd_bf)


def _kvq_kernel(kv_lora, h_ref, cos_ref, sin_ref, gkin_ref, wkv_ref, wkvs_ref, gkv_ref,
                gmix_ref, wdq_ref, gq_ref, wqn_ref, wqr_ref, wqrs_ref, wuk_ref,
                c_ref, kr_ref, kcat_ref, qcat_ref):
    h = h_ref[...]
    cos = cos_ref[...]
    sin = sin_ref[...]
    nb = _rms(h, gkin_ref[...]).astype(BF16)
    kv = _dot(nb, wkv_ref[...])
    kvs = _dot(nb, wkvs_ref[...])
    c = _rms(kv[:, :kv_lora], gkv_ref[...])
    kr = kv[:, kv_lora:] * cos + kvs * sin
    c_ref[...] = c
    kr_ref[...] = kr[:, :QK_ROPE]
    kcat_ref[:, :kv_lora] = c.astype(BF16)
    kcat_ref[:, kv_lora:] = kr.astype(BF16)
    xb = _rms(h, gmix_ref[...]).astype(BF16)
    cq = _rms(_dot(xb, wdq_ref[...]), gq_ref[...]).astype(BF16)
    qn = _dot(cq, wqn_ref[...])
    qr = _dot(cq, wqr_ref[...])
    qrs = _dot(cq, wqrs_ref[...])
    for hh in range(MLA_HEADS):
        qn_h = qn[:, hh * QK_NOPE:(hh + 1) * QK_NOPE].astype(BF16)
        qa_h = _dot(qn_h, wuk_ref[hh]) * SM_SCALE
        sl = slice(hh * ROPE_PAD, (hh + 1) * ROPE_PAD)
        qr_h = (qr[:, sl] * cos + qrs[:, sl] * sin) * SM_SCALE
        qcat_ref[hh, :, :kv_lora] = qa_h.astype(qcat_ref.dtype)
        qcat_ref[hh, :, kv_lora:] = qr_h.astype(qcat_ref.dtype)


def _kvq(h2, cos_t, sin_t, p, q_dtype, tm):
    n, d = h2.shape
    tm = min(tm, cos_t.shape[0])
    kv_lora = p["g_kv"].shape[-1]
    kcw = kv_lora + ROPE_PAD
    nt = cos_t.shape[0] // tm
    const = lambda a: pl.BlockSpec(a.shape, lambda i: (0,) * a.ndim)
    row = lambda w: pl.BlockSpec((tm, w), lambda i: (i, 0))
    tab = pl.BlockSpec((tm, ROPE_PAD), lambda i: (i % nt, 0))
    weights = [p["g_kv_in"], p["w_kv"], p["w_kvs"], p["g_kv"], p["g_mix_b"], p["w_dq"],
               p["g_q"], p["w_qn"], p["w_qr"], p["w_qrs"], p["w_uk"]]
    return pl.pallas_call(
        functools.partial(_kvq_kernel, kv_lora),
        grid=(n // tm,),
        in_specs=[row(d), tab, tab] + [const(w) for w in weights],
        out_specs=[row(kv_lora), row(QK_ROPE), row(kcw),
                   pl.BlockSpec((MLA_HEADS, tm, kcw), lambda i: (0, i, 0))],
        out_shape=[jax.ShapeDtypeStruct((n, kv_lora), F32),
                   jax.ShapeDtypeStruct((n, QK_ROPE), F32),
                   jax.ShapeDtypeStruct((n, kcw), BF16),
                   jax.ShapeDtypeStruct((MLA_HEADS, n, kcw), q_dtype)],
        compiler_params=_params(("parallel",)),
        name="kvq",
    )(h2, cos_t, sin_t, *weights)


def _attn_prompt_kernel(tq, tk, kv_lora, q_ref, k_ref, o_ref, s0_scr, s1_scr, m_scr, l_scr,
                        acc_scr):
    i = pl.program_id(1)
    rows = MLA_HEADS * tq
    s_scrs = (s0_scr, s1_scr)
    n_full = (i * tq) // tk

    m_scr[...] = jnp.full_like(m_scr, NEG_INF)
    l_scr[...] = jnp.zeros_like(l_scr)
    acc_scr[...] = jnp.zeros_like(acc_scr)

    def k_tile(j):
        return k_ref[pl.ds(pl.multiple_of(j * tk, tk), tk), :]

    def stage_a(j, slot):
        q = q_ref[...].reshape(rows, q_ref.shape[-1])
        s_scrs[slot][...] = _dot_nt(q, k_tile(j))

    def stage_b(j, slot, masked):
        s = s_scrs[slot][...]
        if masked:
            q_pos = i * tq + lax.broadcasted_iota(jnp.int32, (rows, tk), 0) % tq
            k_pos = j * tk + lax.broadcasted_iota(jnp.int32, (rows, tk), 1)
            s = jnp.where(k_pos <= q_pos, s, NEG_INF)
        m_prev = m_scr[...]
        m_new = jnp.maximum(m_prev, jnp.max(s, axis=-1, keepdims=True))
        alpha = jnp.exp(m_prev - m_new)
        pr = jnp.exp(s - m_new)
        l_scr[...] = alpha * l_scr[...] + jnp.sum(pr, axis=-1, keepdims=True)
        acc_scr[...] = alpha * acc_scr[...] + _dot(pr.astype(BF16), k_tile(j)[:, :kv_lora])
        m_scr[...] = m_new

    stage_a(0, 0)

    def pair(t, carry):
        stage_a(2 * t + 1, 1)
        stage_b(2 * t, 0, False)
        stage_a(2 * t + 2, 0)
        stage_b(2 * t + 1, 1, False)
        return carry

    lax.fori_loop(0, n_full // 2, pair, 0)

    @pl.when(n_full % 2 == 1)
    def _():
        stage_a(n_full, 1)
        stage_b(n_full - 1, 0, False)
        stage_b(n_full, 1, True)

    @pl.when(n_full % 2 == 0)
    def _():
        stage_b(n_full, 0, True)

    o = acc_scr[...] / l_scr[...]
    o_ref[...] = o.reshape(MLA_HEADS, tq, kv_lora).astype(o_ref.dtype)


def _attn_prompt(qcat, kcat, B, T, kv_lora, tq, tk):
    tk = min(tk, T)
    tq = min(tq, tk)
    assert tk % tq == 0 and T % tk == 0
    nq = T // tq
    kcw = kcat.shape[-1]
    rows = MLA_HEADS * tq
    return pl.pallas_call(
        functools.partial(_attn_prompt_kernel, tq, tk, kv_lora),
        grid=(B, nq),
        in_specs=[pl.BlockSpec((MLA_HEADS, tq, kcw), lambda b, i: (0, b * nq + i, 0)),
                  pl.BlockSpec((T, kcw), lambda b, i: (b, 0))],
        out_specs=pl.BlockSpec((MLA_HEADS, tq, kv_lora), lambda b, i: (0, b * nq + i, 0)),
        out_shape=jax.ShapeDtypeStruct((MLA_HEADS, B * T, kv_lora), BF16),
        scratch_shapes=[pltpu.VMEM((rows, tk), F32), pltpu.VMEM((rows, tk), F32),
                        pltpu.VMEM((rows, 1), F32), pltpu.VMEM((rows, 1), F32),
                        pltpu.VMEM((rows, kv_lora), F32)],
        compiler_params=_params(("parallel", "arbitrary")),
        name="attn_prompt",
    )(qcat, kcat)


def _attn_sample_kernel(gp, gps, kv_lora, pt_ref, q_ref, cn_ref, krn_ref, ckv_hbm, kr_hbm, o_ref,
                        *scr):
    craw, kraw, cbf, kbf, s_scr = scr[0:2], scr[2:4], scr[4:6], scr[6:8], scr[8:10]
    sem_c, sem_k, m_scr, l_scr, acc_scr = scr[10:]
    b = pl.program_id(0)
    nb = pl.num_programs(0)
    n_pages = gp * gps
    t_new = q_ref.shape[1]
    rows = MLA_HEADS * t_new

    def copies(seq, g, slot):
        out = []
        for j in range(gp):
            pid = pt_ref[seq * n_pages + g * gp + j]
            out.append(pltpu.make_async_copy(ckv_hbm.at[pid], craw[slot].at[j], sem_c.at[slot]))
            out.append(pltpu.make_async_copy(kr_hbm.at[pid], kraw[slot].at[j], sem_k.at[slot]))
        return out

    def start(seq, g, slot):
        for c in copies(seq, g, slot):
            c.start()

    def wait(seq, g, slot):
        for c in copies(seq, g, slot):
            c.wait()

    @pl.when(b == 0)
    def _():
        start(0, 0, 0)

    start(b, 1, 1)

    q = q_ref[...].reshape(rows, q_ref.shape[-1])
    qa = q[:, :kv_lora]
    qr = q[:, kv_lora:kv_lora + QK_ROPE]
    cn = cn_ref[...]
    s_new = _dot_nt(qa, cn) + _dot_nt(qr, krn_ref[...])
    q_t = lax.broadcasted_iota(jnp.int32, s_new.shape, 0) % t_new
    k_t = lax.broadcasted_iota(jnp.int32, s_new.shape, 1)
    s_new = jnp.where(k_t <= q_t, s_new, NEG_INF)
    m0 = jnp.max(s_new, axis=-1, keepdims=True)
    p0 = jnp.exp(s_new - m0)
    m_scr[...] = m0
    l_scr[...] = jnp.sum(p0, axis=-1, keepdims=True)
    acc_scr[...] = _dot(p0, cn)
    qa_b = qa.astype(BF16)
    qr_b = qr.astype(BF16)

    def stage_a(slot):
        for j in range(gp):
            cbf[slot][j * PAGE_SIZE:(j + 1) * PAGE_SIZE, :] = craw[slot][j].astype(BF16)
            kbf[slot][:, j * PAGE_SIZE:(j + 1) * PAGE_SIZE] = kraw[slot][j].astype(BF16)
        s_scr[slot][...] = _dot_nt(qa_b, cbf[slot][...]) + _dot(qr_b, kbf[slot][...])

    def stage_b(slot):
        s = s_scr[slot][...]
        m_prev = m_scr[...]
        m_new = jnp.maximum(m_prev, jnp.max(s, axis=-1, keepdims=True))
        alpha = jnp.exp(m_prev - m_new)
        pr = jnp.exp(s - m_new)
        l_scr[...] = alpha * l_scr[...] + jnp.sum(pr, axis=-1, keepdims=True)
        acc_scr[...] = alpha * acc_scr[...] + _dot(pr.astype(BF16), cbf[slot][...])
        m_scr[...] = m_new

    wait(b, 0, 0)
    stage_a(0)
    for g in range(gps - 1):
        if g + 2 < gps:
            start(b, g + 2, g % 2)
        else:
            start(jnp.where(b + 1 < nb, b + 1, 0), 0, 0)
        wait(b, g + 1, (g + 1) % 2)
        stage_a((g + 1) % 2)
        stage_b(g % 2)
    stage_b((gps - 1) % 2)

    o = acc_scr[...] / l_scr[...]
    o_ref[...] = o.reshape(MLA_HEADS, t_new, kv_lora)

    @pl.when(b == nb - 1)
    def _():
        wait(0, 0, 0)


def _attn_sample(qcat, c_new, kr_new, cache_ckv, cache_krope, page_table, gp):
    B, n_pages = page_table.shape
    t_new = c_new.shape[0] // B
    kv_lora = c_new.shape[-1]
    kcw = qcat.shape[-1]
    gp = min(gp, n_pages // 2)
    while n_pages % (2 * gp):
        gp -= 1
    gps = n_pages // gp
    rows = MLA_HEADS * t_new
    keys = gp * PAGE_SIZE
    pt_flat = page_table.reshape(-1)
    krope_t = jnp.swapaxes(cache_krope, 1, 2)
    two = lambda shape, dt: [pltpu.VMEM(shape, dt) for _ in range(2)]
    grid_spec = pltpu.PrefetchScalarGridSpec(
        num_scalar_prefetch=1,
        grid=(B,),
        in_specs=[pl.BlockSpec((MLA_HEADS, t_new, kcw), lambda b, pt: (0, b, 0)),
                  pl.BlockSpec((t_new, kv_lora), lambda b, pt: (b, 0)),
                  pl.BlockSpec((t_new, QK_ROPE), lambda b, pt: (b, 0)),
                  pl.BlockSpec(memory_space=pl.ANY),
                  pl.BlockSpec(memory_space=pl.ANY)],
        out_specs=pl.BlockSpec((MLA_HEADS, t_new, kv_lora), lambda b, pt: (0, b, 0)),
        scratch_shapes=two((gp, PAGE_SIZE, kv_lora), F32) + two((gp, QK_ROPE, PAGE_SIZE), F32)
                       + two((keys, kv_lora), BF16) + two((QK_ROPE, keys), BF16)
                       + two((rows, keys), F32)
                       + [pltpu.SemaphoreType.DMA((2,)), pltpu.SemaphoreType.DMA((2,)),
                          pltpu.VMEM((rows, 1), F32), pltpu.VMEM((rows, 1), F32),
                          pltpu.VMEM((rows, kv_lora), F32)],
    )
    return pl.pallas_call(
        functools.partial(_attn_sample_kernel, gp, gps, kv_lora),
        grid_spec=grid_spec,
        out_shape=jax.ShapeDtypeStruct((MLA_HEADS, B * t_new, kv_lora), F32),
        compiler_params=_params(("arbitrary",)),
        name="attn_sample",
    )(pt_flat, qcat, c_new, kr_new, cache_ckv, krope_t)


def _route_kernel(n_exp, ne8, h_ref, o_ref, wuv_ref, wo_ref, g_ref, wr_ref, tri_ref,
                  h3_ref, hn_ref, comb_ref, rtm_ref, rem_ref, cnt_ref):
    heads = [_dot(o_ref[hh].astype(BF16), wuv_ref[hh]).astype(BF16) for hh in range(MLA_HEADS)]
    h3 = h_ref[...] + _dot(jnp.concatenate(heads, axis=-1), wo_ref[...])
    h3_ref[...] = h3
    hn = _rms(h3, g_ref[...])
    hn_ref[...] = hn.astype(BF16)
    lg = jnp.dot(hn, wr_ref[...], preferred_element_type=F32, precision=lax.Precision.HIGHEST)
    lane = lax.broadcasted_iota(jnp.int32, lg.shape, 1)
    lg = jnp.where(lane < n_exp, lg, -jnp.inf)
    m1 = jnp.max(lg, axis=-1, keepdims=True)
    i1 = jnp.min(jnp.where(lg == m1, lane, LANE), axis=-1, keepdims=True)
    lg2 = jnp.where(lane == i1, -jnp.inf, lg)
    m2 = jnp.max(lg2, axis=-1, keepdims=True)
    i2 = jnp.min(jnp.where(lg2 == m2, lane, LANE), axis=-1, keepdims=True)
    t = jnp.exp(m2 - m1)
    g1 = 1.0 / (1.0 + t)
    comb_ref[...] = jnp.where(lane == i1, g1, 0.0) + jnp.where(lane == i2, t * g1, 0.0)
    routed = jnp.where(lane == i1, 1.0, 0.0) + jnp.where(lane == i2, 1.0, 0.0)
    rank = _dot(tri_ref[...], routed.astype(BF16))
    rtm = jnp.where(routed > 0.0, rank, -1.0)
    rtm_ref[...] = rtm
    rem_ref[...] = rtm.T[:ne8]
    cnt_ref[...] = jnp.max(rtm, axis=0, keepdims=True) + 1.0


def _route(h2, o_lat, p, sb):
    n, d = h2.shape
    n_exp = p["w_e_gate"].shape[0]
    ne8 = -(-n_exp // SUBLANE) * SUBLANE
    kv_lora = o_lat.shape[-1]
    nb = n // sb
    tri = jnp.asarray(np.tril(np.ones((sb, sb), np.float32), -1), BF16)
    c2 = lambda i: (0, 0)
    row = lambda w: pl.BlockSpec((sb, w), lambda i: (i, 0))
    return pl.pallas_call(
        functools.partial(_route_kernel, n_exp, ne8),
        grid=(nb,),
        in_specs=[row(d),
                  pl.BlockSpec((MLA_HEADS, sb, kv_lora), lambda i: (0, i, 0)),
                  pl.BlockSpec(p["w_uv"].shape, lambda i: (0, 0, 0)),
                  pl.BlockSpec((d, d), c2),
                  pl.BlockSpec((1, d), c2),
                  pl.BlockSpec((d, LANE), c2),
                  pl.BlockSpec((sb, sb), c2)],
        out_specs=[row(d), row(d), row(LANE), row(LANE),
                   pl.BlockSpec((None, ne8, sb), lambda i: (i, 0, 0)),
                   pl.BlockSpec((None, 1, LANE), lambda i: (i, 0, 0))],
        out_shape=[jax.ShapeDtypeStruct((n, d), F32),
                   jax.ShapeDtypeStruct((n, d), BF16),
                   jax.ShapeDtypeStruct((n, LANE), F32),
                   jax.ShapeDtypeStruct((n, LANE), F32),
                   jax.ShapeDtypeStruct((nb, ne8, sb), F32),
                   jax.ShapeDtypeStruct((nb, 1, LANE), F32)],
        compiler_params=_params(("parallel",)),
        name="route",
    )(h2, o_lat, p["w_uv"], p["w_out_b"], p["g_ffn1"], p["w_router"], tri)


def _moe_kernel(rt, tmax, cnt_ref, hn_ref, rem_ref, rtm_ref, comb_ref, h3_ref, gfin_ref,
                wg_ref, wu_ref, wd_ref, out_ref, xg_scr, y_scr):
    j = pl.program_id(0)
    e = pl.program_id(1)
    f = pl.program_id(2)
    ne = pl.num_programs(1)
    nf = pl.num_programs(2)
    sb = hn_ref.shape[0]
    n_t = (cnt_ref[j * ne + e] + rt - 1) // rt

    @pl.when((e == 0) & (f == 0))
    def _():
        out_ref[...] = h3_ref[...]

    @pl.when(f == 0)
    def _():
        rank_row = rem_ref[pl.ds(e, 1), :]
        for t in range(tmax):
            @pl.when(t < n_t)
            def _():
                r = (t * rt + lax.broadcasted_iota(jnp.int32, (rt, sb), 0)).astype(F32)
                onehot = jnp.where(rank_row == r, 1.0, 0.0).astype(BF16)
                xg_scr[t] = _dot(onehot, hn_ref[...]).astype(BF16)
                y_scr[t] = jnp.zeros((rt, y_scr.shape[-1]), F32)

    for t in range(tmax):
        @pl.when(t < n_t)
        def _():
            x = xg_scr[t]
            a = _dot(x, wg_ref[...])
            u = _dot(x, wu_ref[...])
            y_scr[t] += _dot((a * _sigmoid(a) * u).astype(BF16), wd_ref[...])

    @pl.when(f == nf - 1)
    def _():
        sel = lax.broadcasted_iota(jnp.int32, (sb, LANE), 1) == e
        rank_col = jnp.sum(jnp.where(sel, rtm_ref[...], 0.0), axis=-1, keepdims=True)
        gate_col = jnp.sum(jnp.where(sel, comb_ref[...], 0.0), axis=-1, keepdims=True)
        for t in range(tmax):
            @pl.when(t < n_t)
            def _():
                r = (t * rt + lax.broadcasted_iota(jnp.int32, (sb, rt), 1)).astype(F32)
                onehot_t = jnp.where(rank_col == r, 1.0, 0.0).astype(BF16)
                out_ref[...] += gate_col * _dot(onehot_t, y_scr[t].astype(BF16))

    @pl.when((e == ne - 1) & (f == nf - 1))
    def _():
        out_ref[...] = _rms(out_ref[...], gfin_ref[...])


def _moe(h3, hn, comb, rtm, rem, cnt, p, sb, rt, tf_target):
    n, d = h3.shape
    n_exp, _, dff = p["w_e_gate"].shape
    tf = _pick_tf(dff, tf_target)
    rt = min(rt, sb)
    tmax = -(-sb // rt)
    ne8 = rem.shape[1]
    once = dict(pipeline_mode=pl.Buffered(1))
    grid_spec = pltpu.PrefetchScalarGridSpec(
        num_scalar_prefetch=1,
        grid=(n // sb, n_exp, dff // tf),
        in_specs=[pl.BlockSpec((sb, d), lambda j, e, f, c: (j, 0), **once),
                  pl.BlockSpec((None, ne8, sb), lambda j, e, f, c: (j, 0, 0), **once),
                  pl.BlockSpec((sb, LANE), lambda j, e, f, c: (j, 0), **once),
                  pl.BlockSpec((sb, LANE), lambda j, e, f, c: (j, 0), **once),
                  pl.BlockSpec((sb, d), lambda j, e, f, c: (j, 0), **once),
                  pl.BlockSpec((1, d), lambda j, e, f, c: (0, 0)),
                  pl.BlockSpec((None, d, tf), lambda j, e, f, c: (e, 0, f)),
                  pl.BlockSpec((None, d, tf), lambda j, e, f, c: (e, 0, f)),
                  pl.BlockSpec((None, tf, d), lambda j, e, f, c: (e, f, 0))],
        out_specs=pl.BlockSpec((sb, d), lambda j, e, f, c: (j, 0)),
        scratch_shapes=[pltpu.VMEM((tmax, rt, d), BF16), pltpu.VMEM((tmax, rt, d), F32)],
    )
    return pl.pallas_call(
        functools.partial(_moe_kernel, rt, tmax),
        grid_spec=grid_spec,
        out_shape=jax.ShapeDtypeStruct((n, d), F32),
        compiler_params=_params(("parallel", "arbitrary", "arbitrary")),
        name="moe",
    )(cnt, hn, rem, rtm, comb, h3, p["g_final"], p["w_e_gate"], p["w_e_up"], p["w_e_down"])


def _rope_tables(pos):
    half = QK_ROPE // 2
    inv = ROPE_THETA ** (-jnp.arange(half, dtype=F32) / half)
    ang = pos.astype(F32)[:, None] * inv[None, :]
    cos, sin = jnp.cos(ang), jnp.sin(ang)
    z = jnp.zeros((pos.shape[0], ROPE_PAD - QK_ROPE), F32)
    return (jnp.concatenate([cos, cos, z], axis=-1), jnp.concatenate([-sin, sin, z], axis=-1))


def _swap_halves(w):
    half = w.shape[-1] // 2
    return jnp.concatenate([w[..., half:], w[..., :half]], axis=-1)


def _pad_rope(w):
    return jnp.pad(w, [(0, 0)] * (w.ndim - 1) + [(0, ROPE_PAD - QK_ROPE)])


def _prepare(g_mix_a, w_in_a, gamma_lb, g_onorm_a, w_out_a, g_kv_in, w_dkv, g_kv, w_ukv,
             g_mix_b, w_dq, g_q, w_uq, w_out_b, g_ffn, w_ff_gate, w_ff_up, w_ff_down,
             w_router, w_e_gate, w_e_up, w_e_down, g_final):
    kv_lora = g_kv.shape[-1]
    q_lora = g_q.shape[-1]
    row = lambda g: g.reshape(1, -1)
    wk_rope = w_dkv[:, kv_lora:]
    w_uq3 = w_uq[0].reshape(q_lora, MLA_HEADS, QK_NOPE + QK_ROPE)
    wq_rope = w_uq3[..., QK_NOPE:]
    w_ukv3 = w_ukv.reshape(kv_lora, MLA_HEADS, QK_NOPE + V_DIM)
    n_exp = w_router.shape[-1]
    return {
        "g_mix_a": g_mix_a[0], "w_in": w_in_a[0].astype(BF16), "gamma_lb": gamma_lb,
        "g_onorm": g_onorm_a[0], "w_out_a": w_out_a[0].astype(BF16),
        "g_ffn0": g_ffn[0], "w_ff_gate": w_ff_gate[0].astype(BF16),
        "w_ff_up": w_ff_up[0].astype(BF16), "w_ff_down": w_ff_down[0].astype(BF16),
        "g_kv_in": row(g_kv_in), "g_kv": row(g_kv), "g_mix_b": row(g_mix_b[0]), "g_q": row(g_q[0]),
        "w_kv": jnp.concatenate([w_dkv[:, :kv_lora], _pad_rope(wk_rope)], axis=-1).astype(BF16),
        "w_kvs": _pad_rope(_swap_halves(wk_rope)).astype(BF16),
        "w_dq": w_dq[0].astype(BF16),
        "w_qn": w_uq3[..., :QK_NOPE].reshape(q_lora, -1).astype(BF16),
        "w_qr": _pad_rope(wq_rope).reshape(q_lora, -1).astype(BF16),
        "w_qrs": _pad_rope(_swap_halves(wq_rope)).reshape(q_lora, -1).astype(BF16),
        "w_uk": jnp.transpose(w_ukv3[..., :QK_NOPE], (1, 2, 0)).astype(BF16),
        "w_uv": jnp.transpose(w_ukv3[..., QK_NOPE:], (1, 0, 2)).astype(BF16),
        "w_out_b": w_out_b[0].astype(BF16),
        "g_ffn1": row(g_ffn[1]), "g_final": row(g_final),
        "w_router": jnp.pad(w_router[0], ((0, 0), (0, LANE - n_exp))),
        "w_e_gate": w_e_gate[0].astype(BF16), "w_e_up": w_e_up[0].astype(BF16),
        "w_e_down": w_e_down[0].astype(BF16),
    }


def _trunk(x, pos, s0, p, attend, q_dtype, gla_bb, tm):
    B, T, d = x.shape
    n = B * T
    x2 = x.reshape(n, d)
    act_dtype = BF16 if T % HGRN_CHUNK == 0 else F32
    q, k, lf, v, sg = _hgrn_proj(x2, p["g_mix_a"], p["gamma_lb"], p["w_in"], 0, tm, act_dtype)
    r3 = lambda a: a.reshape(B, T, d)
    og, s_t = _gla(r3(q), r3(k), r3(lf), r3(v), r3(sg), p["g_onorm"], s0, gla_bb, 512)
    h2 = _ffn(x2, og.reshape(n, d), p["w_out_a"], p["g_ffn0"], p["w_ff_gate"], p["w_ff_up"],
              p["w_ff_down"], tm, 1408)
    cos_t, sin_t = _rope_tables(pos)
    c, kr, kcat, qcat = _kvq(h2, cos_t, sin_t, p, q_dtype, tm)
    o_lat = attend(qcat, kcat, c, kr)
    sb = min(1024, n)
    h3, hn, comb, rtm, rem, cnt = _route(h2, o_lat, p, sb)
    n_exp = p["w_e_gate"].shape[0]
    cnt_i = cnt[:, 0, :n_exp].astype(jnp.int32).reshape(-1)
    y = _moe(h3, hn, comb, rtm, rem, cnt_i, p, sb, 320, 1408)
    kv_lora = c.shape[-1]
    return (y.reshape(B, T, d), c.reshape(B, T, kv_lora), kr.reshape(B, T, QK_ROPE), s_t[None])


def kernel(x_prompt, x_sample, cache_ckv, cache_krope, state_hgrn, page_table, g_mix_a, w_in_a,
           gamma_lb, g_onorm_a, w_out_a, g_kv_in, w_dkv, g_kv, w_ukv, g_mix_b, w_dq, g_q, w_uq,
           w_out_b, g_ffn, w_ff_gate, w_ff_up, w_ff_down, w_router, w_e_gate, w_e_up, w_e_down,
           g_final):
    p = _prepare(g_mix_a, w_in_a, gamma_lb, g_onorm_a, w_out_a, g_kv_in, w_dkv, g_kv, w_ukv,
                 g_mix_b, w_dq, g_q, w_uq, w_out_b, g_ffn, w_ff_gate, w_ff_up, w_ff_down,
                 w_router, w_e_gate, w_e_up, w_e_down, g_final)
    kv_lora = g_kv.shape[-1]
    bp, tp, _ = x_prompt.shape
    bs, ts, _ = x_sample.shape

    def attend_p(qcat, kcat, c, kr):
        return _attn_prompt(qcat, kcat, bp, tp, kv_lora, 128, 512)

    y_p, c_p, kr_p, s_p = _trunk(x_prompt, jnp.arange(tp, dtype=F32), None, p, attend_p,
                                 BF16, 1, 512)

    past = page_table.shape[1] * PAGE_SIZE
    pos_s = jnp.tile(past + jnp.arange(ts, dtype=F32), bs)

    def attend_s(qcat, kcat, c, kr):
        return _attn_sample(qcat, c, kr, cache_ckv, cache_krope, page_table, 8)

    y_s, c_s, kr_s, s_s = _trunk(x_sample, pos_s, state_hgrn[0], p, attend_s, F32, 8, 512)
    return (y_p, y_s, c_p, kr_p, c_s, kr_s, s_p.astype(state_hgrn.dtype),
            s_s.astype(state_hgrn.dtype))
```

```python
import functools

import numpy as np
import jax
import jax.numpy as jnp
from jax import lax
from jax.experimental import pallas as pl
from jax.experimental.pallas import tpu as pltpu

F32 = jnp.float32
BF16 = jnp.bfloat16

RMS_EPS = 1e-6
ROPE_THETA = 10000.0
HGRN_HEADS = 8
HGRN_DK = 128
HGRN_CHUNK = 128
MLA_HEADS = 8
QK_NOPE = 128
QK_ROPE = 64
V_DIM = 128
PAGE_SIZE = 128
TOP_K = 2
SM_SCALE = (QK_NOPE + QK_ROPE) ** -0.5
NEG_INF = -1e30
LANE = 128
SUBLANE = 8
ROPE_PAD = LANE
VMEM_LIMIT = 56 * 1024 * 1024


def _params(sem):
    return pltpu.CompilerParams(dimension_semantics=sem, vmem_limit_bytes=VMEM_LIMIT)


def _dot(a, b):
    return jnp.dot(a, b, preferred_element_type=F32)


def _dot_nt(a, b):
    return lax.dot_general(a, b, (((1,), (1,)), ((), ())), preferred_element_type=F32)


def _dot_tn(a, b):
    return lax.dot_general(a, b, (((0,), (0,)), ((), ())), preferred_element_type=F32)


def _rms(x, g):
    return x * lax.rsqrt(jnp.mean(x * x, axis=-1, keepdims=True) + RMS_EPS) * g


def _sigmoid(x):
    return 1.0 / (1.0 + jnp.exp(-x))


def _lane_fold(x, op):
    w = x.shape[-1]
    if w % LANE or w == LANE:
        return x
    acc = x[:, :LANE]
    for j in range(1, w // LANE):
        acc = op(acc, x[:, j * LANE:(j + 1) * LANE])
    return acc


def _hgrn_proj_kernel(layer, x_ref, g_ref, gam_ref, w_ref, q_ref, k_ref, lf_ref, i_ref, sg_ref):
    d = x_ref.shape[-1]
    xb = _rms(x_ref[...], g_ref[...]).astype(BF16)

    def proj(j):
        return _dot(xb, w_ref[:, j * d:(j + 1) * d])

    gam = gam_ref[...]
    e = jnp.exp(gam - jnp.max(gam, axis=0, keepdims=True))
    lb = jnp.sum(e[:layer + 1], axis=0, keepdims=True) / jnp.sum(e, axis=0, keepdims=True)

    q = proj(0)
    q_ref[...] = (q * _sigmoid(q)).astype(q_ref.dtype)
    z = proj(1)
    t = jnp.exp(-jnp.abs(z))
    r = 1.0 / (1.0 + t)
    pos = z >= 0
    sig_z = jnp.where(pos, r, t * r)
    sig_mz = jnp.where(pos, t * r, r)
    k_ref[...] = ((1.0 - lb) * sig_mz).astype(k_ref.dtype)
    lf_ref[...] = jnp.log(lb + (1.0 - lb) * sig_z)
    i_ref[...] = proj(2).astype(i_ref.dtype)
    g = proj(3)
    sg_ref[...] = (g * _sigmoid(g)).astype(sg_ref.dtype)


def _hgrn_proj(x2, g_mix, gamma_lb, w_in_bf, layer, tm, act_dtype):
    n, d = x2.shape
    tm = min(tm, n)
    row = pl.BlockSpec((tm, d), lambda i: (i, 0))
    out = jax.ShapeDtypeStruct((n, d), act_dtype)
    return pl.pallas_call(
        functools.partial(_hgrn_proj_kernel, layer),
        grid=(n // tm,),
        in_specs=[row,
                  pl.BlockSpec((1, d), lambda i: (0, 0)),
                  pl.BlockSpec(gamma_lb.shape, lambda i: (0, 0)),
                  pl.BlockSpec(w_in_bf.shape, lambda i: (0, 0))],
        out_specs=[row] * 5,
        out_shape=[out, out, jax.ShapeDtypeStruct((n, d), F32), out, out],
        compiler_params=_params(("parallel",)),
        name="hgrn_proj",
    )(x2, g_mix.reshape(1, d), gamma_lb, w_in_bf)


def _gla_constants(L):
    nl = int(np.log2(L))
    assert 2 ** nl == L
    t = np.arange(L)
    me = np.zeros((1 + nl, L, L), np.float32)
    me[0] = (t[None, :] <= t[:, None])
    masks = np.zeros((1 + nl, L, L), np.float32)
    masks[0] = np.eye(L)
    for l in range(nl):
        m = L >> (l + 1)
        blk = t // (2 * m)
        r = blk * 2 * m + m
        upper = t >= r
        j = t[None, :]
        up_rows = (j >= r[:, None]) & (j <= t[:, None])
        lo_rows = (j > t[:, None]) & (j <= r[:, None] - 1)
        me[1 + l] = np.where(upper[:, None], up_rows, lo_rows)
        masks[1 + l] = (upper[:, None] & ~upper[None, :] & (blk[:, None] == blk[None, :]))
    return me.reshape((1 + nl) * L, L), masks, nl


def _gla_kernel(L, nl, nh, has_s0, mm, *refs):
    s_scrs = refs[-nh:]
    refs = refs[:-nh]
    if has_s0:
        (q_ref, k_ref, lf_ref, v_ref, sg_ref, go_ref, me_ref, mk_ref, s0_ref,
         og_ref, st_ref) = refs
    else:
        (q_ref, k_ref, lf_ref, v_ref, sg_ref, go_ref, me_ref, mk_ref,
         og_ref, st_ref) = refs
        s0_ref = None
    bb, tb, _ = q_ref.shape
    nchunk = tb // L
    ti = pl.program_id(1)
    dk = HGRN_DK

    @pl.when(ti == 0)
    def _():
        if has_s0:
            def init(b, carry):
                for h in range(nh):
                    s_scrs[h][b] = s0_ref[b, h].T
                return carry
            lax.fori_loop(0, bb, init, 0)
        else:
            for h in range(nh):
                s_scrs[h][...] = jnp.zeros_like(s_scrs[h])

    me = me_ref[...]
    go = go_ref[...]

    def chunk(b, c):
        sl = pl.ds(pl.multiple_of(c * L, L), L)
        g = lf_ref[b, sl, :]
        if mm == BF16:
            g1 = g.astype(BF16)
            g2 = (g - g1.astype(F32)).astype(BF16)
            ee = _dot(me, g1) + _dot(me, g2)
        else:
            ee = _dot(me, g)
        for h in range(nh):
            hs = slice(h * dk, (h + 1) * dk)
            q = q_ref[b, sl, hs].astype(F32)
            k = k_ref[b, sl, hs].astype(F32)
            v = v_ref[b, sl, hs].astype(mm)
            st = s_scrs[h][b]
            b_cum = ee[0:L, hs]
            b_last = b_cum[L - 1:L, :]
            o = _dot_nt((q * jnp.exp(b_cum)).astype(mm), st.astype(mm))
            sc = _dot_nt(q.astype(mm), k.astype(mm)) * mk_ref[0]
            for l in range(nl):
                e = jnp.exp(ee[(1 + l) * L:(2 + l) * L, hs])
                sc = sc + _dot_nt((q * e).astype(mm), (k * e).astype(mm)) * mk_ref[1 + l]
            o = o + _dot(sc.astype(mm), v)
            kd = (k * jnp.exp(b_last - b_cum)).astype(mm)
            s_scrs[h][b] = st * jnp.exp(b_last) + _dot_tn(v, kd)
            og = _rms(o, go) * sg_ref[b, sl, hs].astype(F32)
            og_ref[b, sl, hs] = og.astype(og_ref.dtype)

    def body(i, carry):
        chunk(i // nchunk, i % nchunk)
        return carry

    lax.fori_loop(0, bb * nchunk, body, 0)

    @pl.when(ti == pl.num_programs(1) - 1)
    def _():
        def fin(b, carry):
            for h in range(nh):
                st_ref[b, h] = s_scrs[h][b].T
            return carry
        lax.fori_loop(0, bb, fin, 0)


def _gla(q, k, lf, v, sg, g_onorm, s0, bb, tb):
    B, T, hd = q.shape
    H = hd // HGRN_DK
    L = HGRN_CHUNK if T % HGRN_CHUNK == 0 else T
    tb = min(tb, T)
    bb = min(bb, B)
    mm = BF16 if L % 16 == 0 else F32
    me_np, mk_np, nl = _gla_constants(L)
    me = jnp.asarray(me_np, mm)
    mk = jnp.asarray(mk_np, F32)
    seq = pl.BlockSpec((bb, tb, hd), lambda b, t: (b, t, 0))
    st_spec = pl.BlockSpec((bb, H, HGRN_DK, HGRN_DK), lambda b, t: (b, 0, 0, 0))
    in_specs = [seq] * 5 + [
        pl.BlockSpec((1, HGRN_DK), lambda b, t: (0, 0)),
        pl.BlockSpec(me.shape, lambda b, t: (0, 0)),
        pl.BlockSpec(mk.shape, lambda b, t: (0, 0, 0)),
    ]
    args = [q, k, lf, v, sg, g_onorm.reshape(1, HGRN_DK), me, mk]
    if s0 is not None:
        in_specs.append(st_spec)
        args.append(s0)
    return pl.pallas_call(
        functools.partial(_gla_kernel, L, nl, H, s0 is not None, mm),
        grid=(B // bb, T // tb),
        in_specs=in_specs,
        out_specs=[seq, st_spec],
        out_shape=[jax.ShapeDtypeStruct((B, T, hd), BF16),
                   jax.ShapeDtypeStruct((B, H, HGRN_DK, HGRN_DK), F32)],
        scratch_shapes=[pltpu.VMEM((bb, HGRN_DK, HGRN_DK), F32) for _ in range(H)],
        compiler_params=_params(("parallel", "arbitrary")),
        name="gla",
    )(*args)


def _ffn_kernel(h_ref, og_ref, wo_ref, g_ref, wg_ref, wu_ref, wd_ref, out_ref,
                h1_scr, hn_scr, acc_scr):
    f = pl.program_id(1)

    @pl.when(f == 0)
    def _():
        h1 = h_ref[...] + _dot(og_ref[...], wo_ref[...])
        h1_scr[...] = h1
        hn_scr[...] = _rms(h1, g_ref[...]).astype(BF16)
        acc_scr[...] = jnp.zeros_like(acc_scr)

    hn = hn_scr[...]
    a = _dot(hn, wg_ref[...])
    u = _dot(hn, wu_ref[...])
    acc_scr[...] += _dot((a * _sigmoid(a) * u).astype(BF16), wd_ref[...])

    @pl.when(f == pl.num_programs(1) - 1)
    def _():
        out_ref[...] = h1_scr[...] + acc_scr[...]


def _pick_tf(dff, target):
    best = None
    for tf in range(LANE, dff + 1, LANE):
        if dff % tf == 0 and tf <= target:
            best = tf
    return best or dff


def _ffn(h, og, w_out_bf, g_ffn, wg_bf, wu_bf, wd_bf, tm, tf_target):
    n, d = h.shape
    dff = wg_bf.shape[1]
    tm = min(tm, n)
    tf = _pick_tf(dff, tf_target)
    row = lambda i, f: (i, 0)
    return pl.pallas_call(
        _ffn_kernel,
        grid=(n // tm, dff // tf),
        in_specs=[pl.BlockSpec((tm, d), row),
                  pl.BlockSpec((tm, d), row),
                  pl.BlockSpec((d, d), lambda i, f: (0, 0)),
                  pl.BlockSpec((1, d), lambda i, f: (0, 0)),
                  pl.BlockSpec((d, tf), lambda i, f: (0, f)),
                  pl.BlockSpec((d, tf), lambda i, f: (0, f)),
                  pl.BlockSpec((tf, d), lambda i, f: (f, 0))],
        out_specs=pl.BlockSpec((tm, d), row),
        out_shape=jax.ShapeDtypeStruct((n, d), F32),
        scratch_shapes=[pltpu.VMEM((tm, d), F32), pltpu.VMEM((tm, d), BF16),
                        pltpu.VMEM((tm, d), F32)],
        compiler_params=_params(("parallel", "arbitrary")),
        name="ffn",
    )(h, og, w_out_bf, g_ffn.reshape(1, d), wg_bf, wu_bf, wd_bf)


def _kvq_kernel(kv_lora, h_ref, cos_ref, sin_ref, gkin_ref, wkv_ref, wkvs_ref, gkv_ref,
                gmix_ref, wdq_ref, gq_ref, wqn_ref, wqr_ref, wqrs_ref, wuk_ref,
                c_ref, kr_ref, kcat_ref, qcat_ref):
    h = h_ref[...]
    cos = cos_ref[...]
    sin = sin_ref[...]
    nb = _rms(h, gkin_ref[...]).astype(BF16)
    kv = _dot(nb, wkv_ref[...])
    kvs = _dot(nb, wkvs_ref[...])
    c = _rms(kv[:, :kv_lora], gkv_ref[...])
    kr = kv[:, kv_lora:] * cos + kvs * sin
    c_ref[...] = c
    kr_ref[...] = kr[:, :QK_ROPE]
    kcat_ref[:, :kv_lora] = c.astype(BF16)
    kcat_ref[:, kv_lora:] = kr.astype(BF16)
    xb = _rms(h, gmix_ref[...]).astype(BF16)
    cq = _rms(_dot(xb, wdq_ref[...]), gq_ref[...]).astype(BF16)
    qn = _dot(cq, wqn_ref[...])
    qr = _dot(cq, wqr_ref[...])
    qrs = _dot(cq, wqrs_ref[...])
    for hh in range(MLA_HEADS):
        qn_h = qn[:, hh * QK_NOPE:(hh + 1) * QK_NOPE].astype(BF16)
        qa_h = _dot(qn_h, wuk_ref[hh]) * SM_SCALE
        sl = slice(hh * ROPE_PAD, (hh + 1) * ROPE_PAD)
        qr_h = (qr[:, sl] * cos + qrs[:, sl] * sin) * SM_SCALE
        qcat_ref[hh, :, :kv_lora] = qa_h.astype(qcat_ref.dtype)
        qcat_ref[hh, :, kv_lora:] = qr_h.astype(qcat_ref.dtype)


def _kvq(h2, cos_t, sin_t, p, q_dtype, tm):
    n, d = h2.shape
    tm = min(tm, cos_t.shape[0])
    kv_lora = p["g_kv"].shape[-1]
    kcw = kv_lora + ROPE_PAD
    nt = cos_t.shape[0] // tm
    const = lambda a: pl.BlockSpec(a.shape, lambda i: (0,) * a.ndim)
    row = lambda w: pl.BlockSpec((tm, w), lambda i: (i, 0))
    tab = pl.BlockSpec((tm, ROPE_PAD), lambda i: (i % nt, 0))
    weights = [p["g_kv_in"], p["w_kv"], p["w_kvs"], p["g_kv"], p["g_mix_b"], p["w_dq"],
               p["g_q"], p["w_qn"], p["w_qr"], p["w_qrs"], p["w_uk"]]
    return pl.pallas_call(
        functools.partial(_kvq_kernel, kv_lora),
        grid=(n // tm,),
        in_specs=[row(d), tab, tab] + [const(w) for w in weights],
        out_specs=[row(kv_lora), row(QK_ROPE), row(kcw),
                   pl.BlockSpec((MLA_HEADS, tm, kcw), lambda i: (0, i, 0))],
        out_shape=[jax.ShapeDtypeStruct((n, kv_lora), F32),
                   jax.ShapeDtypeStruct((n, QK_ROPE), F32),
                   jax.ShapeDtypeStruct((n, kcw), BF16),
                   jax.ShapeDtypeStruct((MLA_HEADS, n, kcw), q_dtype)],
        compiler_params=_params(("parallel",)),
        name="kvq",
    )(h2, cos_t, sin_t, *weights)


def _attn_prompt_kernel(tq, tk, kv_lora, q_ref, k_ref, o_ref, s0_scr, s1_scr, m_scr, l_scr,
                        acc_scr):
    i = pl.program_id(1)
    rows = MLA_HEADS * tq
    s_scrs = (s0_scr, s1_scr)
    n_full = (i * tq) // tk

    m_scr[...] = jnp.full_like(m_scr, NEG_INF)
    l_scr[...] = jnp.zeros_like(l_scr)
    acc_scr[...] = jnp.zeros_like(acc_scr)

    def k_tile(j):
        return k_ref[pl.ds(pl.multiple_of(j * tk, tk), tk), :]

    def stage_a(j, slot):
        q = q_ref[...].reshape(rows, q_ref.shape[-1])
        s_scrs[slot][...] = _dot_nt(q, k_tile(j))

    def stage_b(j, slot, masked):
        s = s_scrs[slot][...]
        if masked:
            q_pos = i * tq + lax.broadcasted_iota(jnp.int32, (rows, tk), 0) % tq
            k_pos = j * tk + lax.broadcasted_iota(jnp.int32, (rows, tk), 1)
            s = jnp.where(k_pos <= q_pos, s, NEG_INF)
        m_prev = m_scr[...]
        m_new = jnp.maximum(m_prev, jnp.max(s, axis=-1, keepdims=True))
        alpha = jnp.exp(m_prev - m_new)
        pr = jnp.exp(s - m_new)
        l_scr[...] = alpha * l_scr[...] + jnp.sum(pr, axis=-1, keepdims=True)
        acc_scr[...] = alpha * acc_scr[...] + _dot(pr.astype(BF16), k_tile(j)[:, :kv_lora])
        m_scr[...] = m_new

    stage_a(0, 0)

    def pair(t, carry):
        stage_a(2 * t + 1, 1)
        stage_b(2 * t, 0, False)
        stage_a(2 * t + 2, 0)
        stage_b(2 * t + 1, 1, False)
        return carry

    lax.fori_loop(0, n_full // 2, pair, 0)

    @pl.when(n_full % 2 == 1)
    def _():
        stage_a(n_full, 1)
        stage_b(n_full - 1, 0, False)
        stage_b(n_full, 1, True)

    @pl.when(n_full % 2 == 0)
    def _():
        stage_b(n_full, 0, True)

    o = acc_scr[...] / l_scr[...]
    o_ref[...] = o.reshape(MLA_HEADS, tq, kv_lora).astype(o_ref.dtype)


def _attn_prompt(qcat, kcat, B, T, kv_lora, tq, tk):
    tk = min(tk, T)
    tq = min(tq, tk)
    assert tk % tq == 0 and T % tk == 0
    nq = T // tq
    kcw = kcat.shape[-1]
    rows = MLA_HEADS * tq
    return pl.pallas_call(
        functools.partial(_attn_prompt_kernel, tq, tk, kv_lora),
        grid=(B, nq),
        in_specs=[pl.BlockSpec((MLA_HEADS, tq, kcw), lambda b, i: (0, b * nq + i, 0)),
                  pl.BlockSpec((T, kcw), lambda b, i: (b, 0))],
        out_specs=pl.BlockSpec((MLA_HEADS, tq, kv_lora), lambda b, i: (0, b * nq + i, 0)),
        out_shape=jax.ShapeDtypeStruct((MLA_HEADS, B * T, kv_lora), BF16),
        scratch_shapes=[pltpu.VMEM((rows, tk), F32), pltpu.VMEM((rows, tk), F32),
                        pltpu.VMEM((rows, 1), F32), pltpu.VMEM((rows, 1), F32),
                        pltpu.VMEM((rows, kv_lora), F32)],
        compiler_params=_params(("parallel", "arbitrary")),
        name="attn_prompt",
    )(qcat, kcat)


def _attn_sample_kernel(gp, gps, ns, kv_lora, pt_ref, q_ref, cn_ref, krn_ref, ckv_hbm, kr_hbm,
                        o_ref, *scr):
    craw, kraw, rest = scr[0:ns], scr[ns:2 * ns], scr[2 * ns:]
    cbf, kbf, s_scr = rest[0:2], rest[2:4], rest[4:6]
    sem_c, sem_k, m_scr, l_scr, acc_scr = rest[6:]
    b = pl.program_id(0)
    nb = pl.num_programs(0)
    total = nb * gps
    base = b * gps
    t_new = q_ref.shape[1]
    rows = MLA_HEADS * t_new

    def copies(grp, slot):
        out = []
        for j in range(gp):
            pid = pt_ref[grp * gp + j]
            out.append(pltpu.make_async_copy(ckv_hbm.at[pid], craw[slot].at[j], sem_c.at[slot]))
            out.append(pltpu.make_async_copy(kr_hbm.at[pid], kraw[slot].at[j], sem_k.at[slot]))
        return out

    def start(grp, slot):
        for c in copies(grp, slot):
            c.start()

    def wait(grp, slot):
        for c in copies(grp, slot):
            c.wait()

    def start_ahead(g):
        nxt = base + g + ns
        start(jnp.where(nxt < total, nxt, nxt - total), g % ns)

    @pl.when(b == 0)
    def _():
        for g in range(ns):
            start(g, g)

    q = q_ref[...].reshape(rows, q_ref.shape[-1])
    qa = q[:, :kv_lora]
    qr = q[:, kv_lora:kv_lora + QK_ROPE]
    cn = cn_ref[...]
    s_new = _dot_nt(qa, cn) + _dot_nt(qr, krn_ref[...])
    q_t = lax.broadcasted_iota(jnp.int32, s_new.shape, 0) % t_new
    k_t = lax.broadcasted_iota(jnp.int32, s_new.shape, 1)
    s_new = jnp.where(k_t <= q_t, s_new, NEG_INF)
    m0 = jnp.max(s_new, axis=-1, keepdims=True)
    p0 = jnp.exp(s_new - m0)
    m_scr[...] = m0
    l_scr[...] = jnp.sum(p0, axis=-1, keepdims=True)
    acc_scr[...] = _dot(p0, cn)
    qa_b = qa.astype(BF16)
    qr_b = qr.astype(BF16)

    def stage_a(raw, slot):
        for j in range(gp):
            cbf[slot][j * PAGE_SIZE:(j + 1) * PAGE_SIZE, :] = craw[raw][j].astype(BF16)
            kbf[slot][:, j * PAGE_SIZE:(j + 1) * PAGE_SIZE] = kraw[raw][j].astype(BF16)
        s_scr[slot][...] = _dot_nt(qa_b, cbf[slot][...]) + _dot(qr_b, kbf[slot][...])

    def stage_b(slot):
        s = s_scr[slot][...]
        m_prev = m_scr[...]
        m_new = jnp.maximum(m_prev, jnp.max(s, axis=-1, keepdims=True))
        alpha = jnp.exp(m_prev - m_new)
        pr = jnp.exp(s - m_new)
        l_scr[...] = alpha * l_scr[...] + jnp.sum(pr, axis=-1, keepdims=True)
        acc_scr[...] = alpha * acc_scr[...] + _dot(pr.astype(BF16), cbf[slot][...])
        m_scr[...] = m_new

    wait(base, 0)
    stage_a(0, 0)
    for g in range(gps - 1):
        start_ahead(g)
        wait(base + g + 1, (g + 1) % ns)
        stage_a((g + 1) % ns, (g + 1) % 2)
        stage_b(g % 2)
    start_ahead(gps - 1)
    stage_b((gps - 1) % 2)

    o = acc_scr[...] / l_scr[...]
    o_ref[...] = o.reshape(MLA_HEADS, t_new, kv_lora)

    @pl.when(b == nb - 1)
    def _():
        for g in range(ns):
            wait(g, g)


def _attn_sample(qcat, c_new, kr_new, cache_ckv, cache_krope, page_table, gp, ns):
    B, n_pages = page_table.shape
    t_new = c_new.shape[0] // B
    kv_lora = c_new.shape[-1]
    kcw = qcat.shape[-1]
    ns = min(ns, n_pages)
    ns -= ns % 2
    gp = min(gp, n_pages // ns)
    while n_pages % (ns * gp):
        gp -= 1
    gps = n_pages // gp
    rows = MLA_HEADS * t_new
    keys = gp * PAGE_SIZE
    pt_flat = page_table.reshape(-1)
    krope_t = jnp.swapaxes(cache_krope, 1, 2)
    many = lambda n, shape, dt: [pltpu.VMEM(shape, dt) for _ in range(n)]
    grid_spec = pltpu.PrefetchScalarGridSpec(
        num_scalar_prefetch=1,
        grid=(B,),
        in_specs=[pl.BlockSpec((MLA_HEADS, t_new, kcw), lambda b, pt: (0, b, 0)),
                  pl.BlockSpec((t_new, kv_lora), lambda b, pt: (b, 0)),
                  pl.BlockSpec((t_new, QK_ROPE), lambda b, pt: (b, 0)),
                  pl.BlockSpec(memory_space=pl.ANY),
                  pl.BlockSpec(memory_space=pl.ANY)],
        out_specs=pl.BlockSpec((MLA_HEADS, t_new, kv_lora), lambda b, pt: (0, b, 0)),
        scratch_shapes=many(ns, (gp, PAGE_SIZE, kv_lora), F32)
                       + many(ns, (gp, QK_ROPE, PAGE_SIZE), F32)
                       + many(2, (keys, kv_lora), BF16) + many(2, (QK_ROPE, keys), BF16)
                       + many(2, (rows, keys), F32)
                       + [pltpu.SemaphoreType.DMA((ns,)), pltpu.SemaphoreType.DMA((ns,)),
                          pltpu.VMEM((rows, 1), F32), pltpu.VMEM((rows, 1), F32),
                          pltpu.VMEM((rows, kv_lora), F32)],
    )
    return pl.pallas_call(
        functools.partial(_attn_sample_kernel, gp, gps, ns, kv_lora),
        grid_spec=grid_spec,
        out_shape=jax.ShapeDtypeStruct((MLA_HEADS, B * t_new, kv_lora), F32),
        compiler_params=_params(("arbitrary",)),
        name="attn_sample",
    )(pt_flat, qcat, c_new, kr_new, cache_ckv, krope_t)


def _route_kernel(n_exp, ne8, h_ref, o_ref, wuv_ref, wo_ref, g_ref, wr_ref, tri_ref,
                  h3_ref, hn_ref, comb_ref, rtm_ref, rem_ref, cnt_ref):
    heads = [_dot(o_ref[hh].astype(BF16), wuv_ref[hh]).astype(BF16) for hh in range(MLA_HEADS)]
    h3 = h_ref[...] + _dot(jnp.concatenate(heads, axis=-1), wo_ref[...])
    h3_ref[...] = h3
    hn = _rms(h3, g_ref[...])
    hn_ref[...] = hn.astype(BF16)
    lg = jnp.dot(hn, wr_ref[...], preferred_element_type=F32, precision=lax.Precision.HIGHEST)
    lane = lax.broadcasted_iota(jnp.int32, lg.shape, 1)
    lg = jnp.where(lane < n_exp, lg, -jnp.inf)
    m1 = jnp.max(lg, axis=-1, keepdims=True)
    i1 = jnp.min(jnp.where(lg == m1, lane, LANE), axis=-1, keepdims=True)
    lg2 = jnp.where(lane == i1, -jnp.inf, lg)
    m2 = jnp.max(lg2, axis=-1, keepdims=True)
    i2 = jnp.min(jnp.where(lg2 == m2, lane, LANE), axis=-1, keepdims=True)
    t = jnp.exp(m2 - m1)
    g1 = 1.0 / (1.0 + t)
    comb_ref[...] = jnp.where(lane == i1, g1, 0.0) + jnp.where(lane == i2, t * g1, 0.0)
    routed = jnp.where(lane == i1, 1.0, 0.0) + jnp.where(lane == i2, 1.0, 0.0)
    rank = _dot(tri_ref[...], routed.astype(BF16))
    rtm = jnp.where(routed > 0.0, rank, -1.0)
    rtm_ref[...] = rtm
    rem_ref[...] = rtm.T[:ne8]
    cnt_ref[...] = jnp.max(rtm, axis=0, keepdims=True) + 1.0


def _route(h2, o_lat, p, sb):
    n, d = h2.shape
    n_exp = p["w_e_gate"].shape[0]
    ne8 = -(-n_exp // SUBLANE) * SUBLANE
    kv_lora = o_lat.shape[-1]
    nb = n // sb
    tri = jnp.asarray(np.tril(np.ones((sb, sb), np.float32), -1), BF16)
    c2 = lambda i: (0, 0)
    row = lambda w: pl.BlockSpec((sb, w), lambda i: (i, 0))
    return pl.pallas_call(
        functools.partial(_route_kernel, n_exp, ne8),
        grid=(nb,),
        in_specs=[row(d),
                  pl.BlockSpec((MLA_HEADS, sb, kv_lora), lambda i: (0, i, 0)),
                  pl.BlockSpec(p["w_uv"].shape, lambda i: (0, 0, 0)),
                  pl.BlockSpec((d, d), c2),
                  pl.BlockSpec((1, d), c2),
                  pl.BlockSpec((d, LANE), c2),
                  pl.BlockSpec((sb, sb), c2)],
        out_specs=[row(d), row(d), row(LANE), row(LANE),
                   pl.BlockSpec((None, ne8, sb), lambda i: (i, 0, 0)),
                   pl.BlockSpec((None, 1, LANE), lambda i: (i, 0, 0))],
        out_shape=[jax.ShapeDtypeStruct((n, d), F32),
                   jax.ShapeDtypeStruct((n, d), BF16),
                   jax.ShapeDtypeStruct((n, LANE), F32),
                   jax.ShapeDtypeStruct((n, LANE), F32),
                   jax.ShapeDtypeStruct((nb, ne8, sb), F32),
                   jax.ShapeDtypeStruct((nb, 1, LANE), F32)],
        compiler_params=_params(("parallel",)),
        name="route",
    )(h2, o_lat, p["w_uv"], p["w_out_b"], p["g_ffn1"], p["w_router"], tri)


def _moe_kernel(rt, tmax, cnt_ref, hn_ref, rem_ref, rtm_ref, comb_ref, h3_ref, gfin_ref,
                wg_ref, wu_ref, wd_ref, out_ref, xg_scr, y_scr):
    j = pl.program_id(0)
    e = pl.program_id(1)
    f = pl.program_id(2)
    ne = pl.num_programs(1)
    nf = pl.num_programs(2)
    sb = hn_ref.shape[0]
    n_t = (cnt_ref[j * ne + e] + rt - 1) // rt

    @pl.when((e == 0) & (f == 0))
    def _():
        out_ref[...] = h3_ref[...]

    @pl.when(f == 0)
    def _():
        rank_row = rem_ref[pl.ds(e, 1), :]
        for t in range(tmax):
            @pl.when(t < n_t)
            def _():
                r = (t * rt + lax.broadcasted_iota(jnp.int32, (rt, sb), 0)).astype(F32)
                onehot = jnp.where(rank_row == r, 1.0, 0.0).astype(BF16)
                xg_scr[t] = _dot(onehot, hn_ref[...]).astype(BF16)
                y_scr[t] = jnp.zeros((rt, y_scr.shape[-1]), F32)

    for t in range(tmax):
        @pl.when(t < n_t)
        def _():
            x = xg_scr[t]
            a = _dot(x, wg_ref[...])
            u = _dot(x, wu_ref[...])
            y_scr[t] += _dot((a * _sigmoid(a) * u).astype(BF16), wd_ref[...])

    @pl.when(f == nf - 1)
    def _():
        sel = lax.broadcasted_iota(jnp.int32, (sb, LANE), 1) == e
        rank_col = jnp.sum(jnp.where(sel, rtm_ref[...], 0.0), axis=-1, keepdims=True)
        gate_col = jnp.sum(jnp.where(sel, comb_ref[...], 0.0), axis=-1, keepdims=True)
        for t in range(tmax):
            @pl.when(t < n_t)
            def _():
                r = (t * rt + lax.broadcasted_iota(jnp.int32, (sb, rt), 1)).astype(F32)
                onehot_t = jnp.where(rank_col == r, 1.0, 0.0).astype(BF16)
                out_ref[...] += gate_col * _dot(onehot_t, y_scr[t].astype(BF16))

    @pl.when((e == ne - 1) & (f == nf - 1))
    def _():
        out_ref[...] = _rms(out_ref[...], gfin_ref[...])


def _moe(h3, hn, comb, rtm, rem, cnt, p, sb, rt, tf_target):
    n, d = h3.shape
    n_exp, _, dff = p["w_e_gate"].shape
    tf = _pick_tf(dff, tf_target)
    rt = min(rt, sb)
    tmax = -(-sb // rt)
    ne8 = rem.shape[1]
    once = dict(pipeline_mode=pl.Buffered(1))
    grid_spec = pltpu.PrefetchScalarGridSpec(
        num_scalar_prefetch=1,
        grid=(n // sb, n_exp, dff // tf),
        in_specs=[pl.BlockSpec((sb, d), lambda j, e, f, c: (j, 0), **once),
                  pl.BlockSpec((None, ne8, sb), lambda j, e, f, c: (j, 0, 0), **once),
                  pl.BlockSpec((sb, LANE), lambda j, e, f, c: (j, 0), **once),
                  pl.BlockSpec((sb, LANE), lambda j, e, f, c: (j, 0), **once),
                  pl.BlockSpec((sb, d), lambda j, e, f, c: (j, 0), **once),
                  pl.BlockSpec((1, d), lambda j, e, f, c: (0, 0)),
                  pl.BlockSpec((None, d, tf), lambda j, e, f, c: (e, 0, f)),
                  pl.BlockSpec((None, d, tf), lambda j, e, f, c: (e, 0, f)),
                  pl.BlockSpec((None, tf, d), lambda j, e, f, c: (e, f, 0))],
        out_specs=pl.BlockSpec((sb, d), lambda j, e, f, c: (j, 0)),
        scratch_shapes=[pltpu.VMEM((tmax, rt, d), BF16), pltpu.VMEM((tmax, rt, d), F32)],
    )
    return pl.pallas_call(
        functools.partial(_moe_kernel, rt, tmax),
        grid_spec=grid_spec,
        out_shape=jax.ShapeDtypeStruct((n, d), F32),
        compiler_params=_params(("parallel", "arbitrary", "arbitrary")),
        name="moe",
    )(cnt, hn, rem, rtm, comb, h3, p["g_final"], p["w_e_gate"], p["w_e_up"], p["w_e_down"])


def _rope_tables(pos):
    half = QK_ROPE // 2
    inv = ROPE_THETA ** (-jnp.arange(half, dtype=F32) / half)
    ang = pos.astype(F32)[:, None] * inv[None, :]
    cos, sin = jnp.cos(ang), jnp.sin(ang)
    z = jnp.zeros((pos.shape[0], ROPE_PAD - QK_ROPE), F32)
    return (jnp.concatenate([cos, cos, z], axis=-1), jnp.concatenate([-sin, sin, z], axis=-1))


def _swap_halves(w):
    half = w.shape[-1] // 2
    return jnp.concatenate([w[..., half:], w[..., :half]], axis=-1)


def _pad_rope(w):
    return jnp.pad(w, [(0, 0)] * (w.ndim - 1) + [(0, ROPE_PAD - QK_ROPE)])


def _prepare(g_mix_a, w_in_a, gamma_lb, g_onorm_a, w_out_a, g_kv_in, w_dkv, g_kv, w_ukv,
             g_mix_b, w_dq, g_q, w_uq, w_out_b, g_ffn, w_ff_gate, w_ff_up, w_ff_down,
             w_router, w_e_gate, w_e_up, w_e_down, g_final):
    kv_lora = g_kv.shape[-1]
    q_lora = g_q.shape[-1]
    row = lambda g: g.reshape(1, -1)
    wk_rope = w_dkv[:, kv_lora:]
    w_uq3 = w_uq[0].reshape(q_lora, MLA_HEADS, QK_NOPE + QK_ROPE)
    wq_rope = w_uq3[..., QK_NOPE:]
    w_ukv3 = w_ukv.reshape(kv_lora, MLA_HEADS, QK_NOPE + V_DIM)
    n_exp = w_router.shape[-1]
    return {
        "g_mix_a": g_mix_a[0], "w_in": w_in_a[0].astype(BF16), "gamma_lb": gamma_lb,
        "g_onorm": g_onorm_a[0], "w_out_a": w_out_a[0].astype(BF16),
        "g_ffn0": g_ffn[0], "w_ff_gate": w_ff_gate[0].astype(BF16),
        "w_ff_up": w_ff_up[0].astype(BF16), "w_ff_down": w_ff_down[0].astype(BF16),
        "g_kv_in": row(g_kv_in), "g_kv": row(g_kv), "g_mix_b": row(g_mix_b[0]), "g_q": row(g_q[0]),
        "w_kv": jnp.concatenate([w_dkv[:, :kv_lora], _pad_rope(wk_rope)], axis=-1).astype(BF16),
        "w_kvs": _pad_rope(_swap_halves(wk_rope)).astype(BF16),
        "w_dq": w_dq[0].astype(BF16),
        "w_qn": w_uq3[..., :QK_NOPE].reshape(q_lora, -1).astype(BF16),
        "w_qr": _pad_rope(wq_rope).reshape(q_lora, -1).astype(BF16),
        "w_qrs": _pad_rope(_swap_halves(wq_rope)).reshape(q_lora, -1).astype(BF16),
        "w_uk": jnp.transpose(w_ukv3[..., :QK_NOPE], (1, 2, 0)).astype(BF16),
        "w_uv": jnp.transpose(w_ukv3[..., QK_NOPE:], (1, 0, 2)).astype(BF16),
        "w_out_b": w_out_b[0].astype(BF16),
        "g_ffn1": row(g_ffn[1]), "g_final": row(g_final),
        "w_router": jnp.pad(w_router[0], ((0, 0), (0, LANE - n_exp))),
        "w_e_gate": w_e_gate[0].astype(BF16), "w_e_up": w_e_up[0].astype(BF16),
        "w_e_down": w_e_down[0].astype(BF16),
    }


def _trunk(x, pos, s0, p, attend, q_dtype, gla_bb, tm):
    B, T, d = x.shape
    n = B * T
    x2 = x.reshape(n, d)
    act_dtype = BF16 if T % HGRN_CHUNK == 0 else F32
    q, k, lf, v, sg = _hgrn_proj(x2, p["g_mix_a"], p["gamma_lb"], p["w_in"], 0, tm, act_dtype)
    r3 = lambda a: a.reshape(B, T, d)
    og, s_t = _gla(r3(q), r3(k), r3(lf), r3(v), r3(sg), p["g_onorm"], s0, gla_bb, 512)
    h2 = _ffn(x2, og.reshape(n, d), p["w_out_a"], p["g_ffn0"], p["w_ff_gate"], p["w_ff_up"],
              p["w_ff_down"], tm, 1408)
    cos_t, sin_t = _rope_tables(pos)
    c, kr, kcat, qcat = _kvq(h2, cos_t, sin_t, p, q_dtype, tm)
    o_lat = attend(qcat, kcat, c, kr)
    sb = min(1024, n)
    h3, hn, comb, rtm, rem, cnt = _route(h2, o_lat, p, sb)
    n_exp = p["w_e_gate"].shape[0]
    cnt_i = cnt[:, 0, :n_exp].astype(jnp.int32).reshape(-1)
    y = _moe(h3, hn, comb, rtm, rem, cnt_i, p, sb, 288, 1408)
    kv_lora = c.shape[-1]
    return (y.reshape(B, T, d), c.reshape(B, T, kv_lora), kr.reshape(B, T, QK_ROPE), s_t[None])


def kernel(x_prompt, x_sample, cache_ckv, cache_krope, state_hgrn, page_table, g_mix_a, w_in_a,
           gamma_lb, g_onorm_a, w_out_a, g_kv_in, w_dkv, g_kv, w_ukv, g_mix_b, w_dq, g_q, w_uq,
           w_out_b, g_ffn, w_ff_gate, w_ff_up, w_ff_down, w_router, w_e_gate, w_e_up, w_e_down,
           g_final):
    p = _prepare(g_mix_a, w_in_a, gamma_lb, g_onorm_a, w_out_a, g_kv_in, w_dkv, g_kv, w_ukv,
                 g_mix_b, w_dq, g_q, w_uq, w_out_b, g_ffn, w_ff_gate, w_ff_up, w_ff_down,
                 w_router, w_e_gate, w_e_up, w_e_down, g_final)
    kv_lora = g_kv.shape[-1]
    bp, tp, _ = x_prompt.shape
    bs, ts, _ = x_sample.shape

    def attend_p(qcat, kcat, c, kr):
        return _attn_prompt(qcat, kcat, bp, tp, kv_lora, 128, 512)

    y_p, c_p, kr_p, s_p = _trunk(x_prompt, jnp.arange(tp, dtype=F32), None, p, attend_p,
                                 BF16, 1, 512)

    past = page_table.shape[1] * PAGE_SIZE
    pos_s = jnp.tile(past + jnp.arange(ts, dtype=F32), bs)

    def attend_s(qcat, kcat, c, kr):
        return _attn_sample(qcat, c, kr, cache_ckv, cache_krope, page_table, 8, 4)

    y_s, c_s, kr_s, s_s = _trunk(x_sample, pos_s, state_hgrn[0], p, attend_s, F32, 8, 512)
    return (y_p, y_s, c_p, kr_p, c_s, kr_s, s_p.astype(state_hgrn.dtype),
            s_s.astype(state_hgrn.dtype))
```

```python
import functools

import numpy as np
import jax
import jax.numpy as jnp
from jax import lax
from jax.experimental import pallas as pl
from jax.experimental.pallas import tpu as pltpu

F32 = jnp.float32
BF16 = jnp.bfloat16

RMS_EPS = 1e-6
ROPE_THETA = 10000.0
HGRN_HEADS = 8
HGRN_DK = 128
HGRN_CHUNK = 128
MLA_HEADS = 8
QK_NOPE = 128
QK_ROPE = 64
V_DIM = 128
PAGE_SIZE = 128
TOP_K = 2
SM_SCALE = (QK_NOPE + QK_ROPE) ** -0.5
NEG_INF = -1e30
LANE = 128
SUBLANE = 8
ROPE_PAD = LANE
VMEM_LIMIT = 56 * 1024 * 1024
FF_CHUNK = 1408


def _params(sem):
    return pltpu.CompilerParams(dimension_semantics=sem, vmem_limit_bytes=VMEM_LIMIT)


def _dot(a, b):
    return jnp.dot(a, b, preferred_element_type=F32)


def _dot_nt(a, b):
    return lax.dot_general(a, b, (((1,), (1,)), ((), ())), preferred_element_type=F32)


def _dot_tn(a, b):
    return lax.dot_general(a, b, (((0,), (0,)), ((), ())), preferred_element_type=F32)


def _rms(x, g):
    return x * lax.rsqrt(jnp.mean(x * x, axis=-1, keepdims=True) + RMS_EPS) * g


def _sigmoid(x):
    return 1.0 / (1.0 + jnp.exp(-x))


def _lane_fold(x, op):
    w = x.shape[-1]
    if w % LANE or w == LANE:
        return x
    acc = x[:, :LANE]
    for j in range(1, w // LANE):
        acc = op(acc, x[:, j * LANE:(j + 1) * LANE])
    return acc


def _hgrn_proj_kernel(layer, x_ref, g_ref, gam_ref, w_ref, q_ref, k_ref, lf_ref, i_ref, sg_ref):
    d = x_ref.shape[-1]
    xb = _rms(x_ref[...], g_ref[...]).astype(BF16)

    def proj(j):
        return _dot(xb, w_ref[:, j * d:(j + 1) * d])

    gam = gam_ref[...]
    e = jnp.exp(gam - jnp.max(gam, axis=0, keepdims=True))
    lb = jnp.sum(e[:layer + 1], axis=0, keepdims=True) / jnp.sum(e, axis=0, keepdims=True)

    q = proj(0)
    q_ref[...] = (q * _sigmoid(q)).astype(q_ref.dtype)
    z = proj(1)
    t = jnp.exp(-jnp.abs(z))
    r = 1.0 / (1.0 + t)
    pos = z >= 0
    sig_z = jnp.where(pos, r, t * r)
    sig_mz = jnp.where(pos, t * r, r)
    k_ref[...] = ((1.0 - lb) * sig_mz).astype(k_ref.dtype)
    lf_ref[...] = jnp.log(lb + (1.0 - lb) * sig_z)
    i_ref[...] = proj(2).astype(i_ref.dtype)
    g = proj(3)
    sg_ref[...] = (g * _sigmoid(g)).astype(sg_ref.dtype)


def _hgrn_proj(x2, g_mix, gamma_lb, w_in_bf, layer, tm, act_dtype):
    n, d = x2.shape
    tm = min(tm, n)
    row = pl.BlockSpec((tm, d), lambda i: (i, 0))
    out = jax.ShapeDtypeStruct((n, d), act_dtype)
    return pl.pallas_call(
        functools.partial(_hgrn_proj_kernel, layer),
        grid=(n // tm,),
        in_specs=[row,
                  pl.BlockSpec((1, d), lambda i: (0, 0)),
                  pl.BlockSpec(gamma_lb.shape, lambda i: (0, 0)),
                  pl.BlockSpec(w_in_bf.shape, lambda i: (0, 0))],
        out_specs=[row] * 5,
        out_shape=[out, out, jax.ShapeDtypeStruct((n, d), F32), out, out],
        compiler_params=_params(("parallel",)),
        name="hgrn_proj",
    )(x2, g_mix.reshape(1, d), gamma_lb, w_in_bf)


def _gla_constants(L):
    nl = int(np.log2(L))
    assert 2 ** nl == L
    t = np.arange(L)
    me = np.zeros((1 + nl, L, L), np.float32)
    me[0] = (t[None, :] <= t[:, None])
    masks = np.zeros((1 + nl, L, L), np.float32)
    masks[0] = np.eye(L)
    for l in range(nl):
        m = L >> (l + 1)
        blk = t // (2 * m)
        r = blk * 2 * m + m
        upper = t >= r
        j = t[None, :]
        up_rows = (j >= r[:, None]) & (j <= t[:, None])
        lo_rows = (j > t[:, None]) & (j <= r[:, None] - 1)
        me[1 + l] = np.where(upper[:, None], up_rows, lo_rows)
        masks[1 + l] = (upper[:, None] & ~upper[None, :] & (blk[:, None] == blk[None, :]))
    return me.reshape((1 + nl) * L, L), masks, nl


def _gla_kernel(L, nl, nh, has_s0, mm, *refs):
    s_scrs = refs[-nh:]
    refs = refs[:-nh]
    if has_s0:
        (q_ref, k_ref, lf_ref, v_ref, sg_ref, go_ref, me_ref, mk_ref, s0_ref,
         og_ref, st_ref) = refs
    else:
        (q_ref, k_ref, lf_ref, v_ref, sg_ref, go_ref, me_ref, mk_ref,
         og_ref, st_ref) = refs
        s0_ref = None
    bb, tb, _ = q_ref.shape
    nchunk = tb // L
    ti = pl.program_id(1)
    dk = HGRN_DK

    @pl.when(ti == 0)
    def _():
        if has_s0:
            def init(b, carry):
                for h in range(nh):
                    s_scrs[h][b] = s0_ref[b, h].T
                return carry
            lax.fori_loop(0, bb, init, 0)
        else:
            for h in range(nh):
                s_scrs[h][...] = jnp.zeros_like(s_scrs[h])

    me = me_ref[...]
    go = go_ref[...]

    def chunk(b, c):
        sl = pl.ds(pl.multiple_of(c * L, L), L)
        g = lf_ref[b, sl, :]
        if mm == BF16:
            g1 = g.astype(BF16)
            g2 = (g - g1.astype(F32)).astype(BF16)
            ee = _dot(me, g1) + _dot(me, g2)
        else:
            ee = _dot(me, g)
        for h in range(nh):
            hs = slice(h * dk, (h + 1) * dk)
            q = q_ref[b, sl, hs].astype(F32)
            k = k_ref[b, sl, hs].astype(F32)
            v = v_ref[b, sl, hs].astype(mm)
            st = s_scrs[h][b]
            b_cum = ee[0:L, hs]
            b_last = b_cum[L - 1:L, :]
            o = _dot_nt((q * jnp.exp(b_cum)).astype(mm), st.astype(mm))
            sc = _dot_nt(q.astype(mm), k.astype(mm)) * mk_ref[0]
            for l in range(nl):
                e = jnp.exp(ee[(1 + l) * L:(2 + l) * L, hs])
                sc = sc + _dot_nt((q * e).astype(mm), (k * e).astype(mm)) * mk_ref[1 + l]
            o = o + _dot(sc.astype(mm), v)
            kd = (k * jnp.exp(b_last - b_cum)).astype(mm)
            s_scrs[h][b] = st * jnp.exp(b_last) + _dot_tn(v, kd)
            og = _rms(o, go) * sg_ref[b, sl, hs].astype(F32)
            og_ref[b, sl, hs] = og.astype(og_ref.dtype)

    def body(i, carry):
        chunk(i // nchunk, i % nchunk)
        return carry

    lax.fori_loop(0, bb * nchunk, body, 0)

    @pl.when(ti == pl.num_programs(1) - 1)
    def _():
        def fin(b, carry):
            for h in range(nh):
                st_ref[b, h] = s_scrs[h][b].T
            return carry
        lax.fori_loop(0, bb, fin, 0)


def _gla(q, k, lf, v, sg, g_onorm, s0, bb, tb):
    B, T, hd = q.shape
    H = hd // HGRN_DK
    L = HGRN_CHUNK if T % HGRN_CHUNK == 0 else T
    tb = min(tb, T)
    bb = min(bb, B)
    mm = BF16 if L % 16 == 0 else F32
    me_np, mk_np, nl = _gla_constants(L)
    me = jnp.asarray(me_np, mm)
    mk = jnp.asarray(mk_np, F32)
    seq = pl.BlockSpec((bb, tb, hd), lambda b, t: (b, t, 0))
    st_spec = pl.BlockSpec((bb, H, HGRN_DK, HGRN_DK), lambda b, t: (b, 0, 0, 0))
    in_specs = [seq] * 5 + [
        pl.BlockSpec((1, HGRN_DK), lambda b, t: (0, 0)),
        pl.BlockSpec(me.shape, lambda b, t: (0, 0)),
        pl.BlockSpec(mk.shape, lambda b, t: (0, 0, 0)),
    ]
    args = [q, k, lf, v, sg, g_onorm.reshape(1, HGRN_DK), me, mk]
    if s0 is not None:
        in_specs.append(st_spec)
        args.append(s0)
    return pl.pallas_call(
        functools.partial(_gla_kernel, L, nl, H, s0 is not None, mm),
        grid=(B // bb, T // tb),
        in_specs=in_specs,
        out_specs=[seq, st_spec],
        out_shape=[jax.ShapeDtypeStruct((B, T, hd), BF16),
                   jax.ShapeDtypeStruct((B, H, HGRN_DK, HGRN_DK), F32)],
        scratch_shapes=[pltpu.VMEM((bb, HGRN_DK, HGRN_DK), F32) for _ in range(H)],
        compiler_params=_params(("parallel", "arbitrary")),
        name="gla",
    )(*args)


def _ffn_kernel(h_ref, og_ref, wo_ref, g_ref, wg_ref, wu_ref, wd_ref, out_ref,
                h1_scr, hn_scr, acc_scr):
    f = pl.program_id(1)

    @pl.when(f == 0)
    def _():
        h1 = h_ref[...] + _dot(og_ref[...], wo_ref[...])
        h1_scr[...] = h1
        hn_scr[...] = _rms(h1, g_ref[...]).astype(BF16)
        acc_scr[...] = jnp.zeros_like(acc_scr)

    hn = hn_scr[...]
    a = _dot(hn, wg_ref[...])
    u = _dot(hn, wu_ref[...])
    acc_scr[...] += _dot((a * _sigmoid(a) * u).astype(BF16), wd_ref[...])

    @pl.when(f == pl.num_programs(1) - 1)
    def _():
        out_ref[...] = h1_scr[...] + acc_scr[...]


def _pick_tf(dff, target):
    best = None
    for tf in range(LANE, dff + 1, LANE):
        if dff % tf == 0 and tf <= target:
            best = tf
    return best or dff


def _chunk_cols(w, tf):
    *lead, d, dff = w.shape
    return jnp.moveaxis(w.reshape(*lead, d, dff // tf, tf), -2, -3)


def _ffn(h, og, w_out_bf, g_ffn, wg_bf, wu_bf, wd_bf, tm):
    n, d = h.shape
    nf, _, tf = wg_bf.shape
    tm = min(tm, n)
    row = lambda i, f: (i, 0)
    return pl.pallas_call(
        _ffn_kernel,
        grid=(n // tm, nf),
        in_specs=[pl.BlockSpec((tm, d), row),
                  pl.BlockSpec((tm, d), row),
                  pl.BlockSpec((d, d), lambda i, f: (0, 0)),
                  pl.BlockSpec((1, d), lambda i, f: (0, 0)),
                  pl.BlockSpec((None, d, tf), lambda i, f: (f, 0, 0)),
                  pl.BlockSpec((None, d, tf), lambda i, f: (f, 0, 0)),
                  pl.BlockSpec((tf, d), lambda i, f: (f, 0))],
        out_specs=pl.BlockSpec((tm, d), row),
        out_shape=jax.ShapeDtypeStruct((n, d), F32),
        scratch_shapes=[pltpu.VMEM((tm, d), F32), pltpu.VMEM((tm, d), BF16),
                        pltpu.VMEM((tm, d), F32)],
        compiler_params=_params(("parallel", "arbitrary")),
        name="ffn",
    )(h, og, w_out_bf, g_ffn.reshape(1, d), wg_bf, wu_bf, wd_bf)


def _kvq_kernel(kv_lora, h_ref, cos_ref, sin_ref, gkin_ref, wkv_ref, wkvs_ref, gkv_ref,
                gmix_ref, wdq_ref, gq_ref, wqn_ref, wqr_ref, wqrs_ref, wuk_ref,
                c_ref, kr_ref, kcat_ref, qcat_ref):
    h = h_ref[...]
    cos = cos_ref[...]
    sin = sin_ref[...]
    nb = _rms(h, gkin_ref[...]).astype(BF16)
    kv = _dot(nb, wkv_ref[...])
    kvs = _dot(nb, wkvs_ref[...])
    c = _rms(kv[:, :kv_lora], gkv_ref[...])
    kr = kv[:, kv_lora:] * cos + kvs * sin
    c_ref[...] = c
    kr_ref[...] = kr[:, :QK_ROPE]
    kcat_ref[:, :kv_lora] = c.astype(BF16)
    kcat_ref[:, kv_lora:] = kr.astype(BF16)
    xb = _rms(h, gmix_ref[...]).astype(BF16)
    cq = _rms(_dot(xb, wdq_ref[...]), gq_ref[...]).astype(BF16)
    qn = _dot(cq, wqn_ref[...])
    qr = _dot(cq, wqr_ref[...])
    qrs = _dot(cq, wqrs_ref[...])
    for hh in range(MLA_HEADS):
        qn_h = qn[:, hh * QK_NOPE:(hh + 1) * QK_NOPE].astype(BF16)
        qa_h = _dot(qn_h, wuk_ref[hh]) * SM_SCALE
        sl = slice(hh * ROPE_PAD, (hh + 1) * ROPE_PAD)
        qr_h = (qr[:, sl] * cos + qrs[:, sl] * sin) * SM_SCALE
        qcat_ref[hh, :, :kv_lora] = qa_h.astype(qcat_ref.dtype)
        qcat_ref[hh, :, kv_lora:] = qr_h.astype(qcat_ref.dtype)


def _kvq(h2, cos_t, sin_t, p, q_dtype, tm):
    n, d = h2.shape
    tm = min(tm, cos_t.shape[0])
    kv_lora = p["g_kv"].shape[-1]
    kcw = kv_lora + ROPE_PAD
    nt = cos_t.shape[0] // tm
    const = lambda a: pl.BlockSpec(a.shape, lambda i: (0,) * a.ndim)
    row = lambda w: pl.BlockSpec((tm, w), lambda i: (i, 0))
    tab = pl.BlockSpec((tm, ROPE_PAD), lambda i: (i % nt, 0))
    weights = [p["g_kv_in"], p["w_kv"], p["w_kvs"], p["g_kv"], p["g_mix_b"], p["w_dq"],
               p["g_q"], p["w_qn"], p["w_qr"], p["w_qrs"], p["w_uk"]]
    return pl.pallas_call(
        functools.partial(_kvq_kernel, kv_lora),
        grid=(n // tm,),
        in_specs=[row(d), tab, tab] + [const(w) for w in weights],
        out_specs=[row(kv_lora), row(QK_ROPE), row(kcw),
                   pl.BlockSpec((MLA_HEADS, tm, kcw), lambda i: (0, i, 0))],
        out_shape=[jax.ShapeDtypeStruct((n, kv_lora), F32),
                   jax.ShapeDtypeStruct((n, QK_ROPE), F32),
                   jax.ShapeDtypeStruct((n, kcw), BF16),
                   jax.ShapeDtypeStruct((MLA_HEADS, n, kcw), q_dtype)],
        compiler_params=_params(("parallel",)),
        name="kvq",
    )(h2, cos_t, sin_t, *weights)


def _attn_prompt_kernel(tq, tk, kv_lora, q_ref, k_ref, o_ref, s0_scr, s1_scr, m_scr, l_scr,
                        acc_scr):
    i = pl.program_id(1)
    rows = MLA_HEADS * tq
    s_scrs = (s0_scr, s1_scr)
    n_full = (i * tq) // tk

    m_scr[...] = jnp.full_like(m_scr, NEG_INF)
    l_scr[...] = jnp.zeros_like(l_scr)
    acc_scr[...] = jnp.zeros_like(acc_scr)

    def k_tile(j):
        return k_ref[pl.ds(pl.multiple_of(j * tk, tk), tk), :]

    def stage_a(j, slot):
        q = q_ref[...].reshape(rows, q_ref.shape[-1])
        s_scrs[slot][...] = _dot_nt(q, k_tile(j))

    def stage_b(j, slot, masked):
        s = s_scrs[slot][...]
        if masked:
            q_pos = i * tq + lax.broadcasted_iota(jnp.int32, (rows, tk), 0) % tq
            k_pos = j * tk + lax.broadcasted_iota(jnp.int32, (rows, tk), 1)
            s = jnp.where(k_pos <= q_pos, s, NEG_INF)
        m_prev = m_scr[...]
        m_new = jnp.maximum(m_prev, jnp.max(s, axis=-1, keepdims=True))
        alpha = jnp.exp(m_prev - m_new)
        pr = jnp.exp(s - m_new)
        l_scr[...] = alpha * l_scr[...] + jnp.sum(pr, axis=-1, keepdims=True)
        acc_scr[...] = alpha * acc_scr[...] + _dot(pr.astype(BF16), k_tile(j)[:, :kv_lora])
        m_scr[...] = m_new

    stage_a(0, 0)

    def pair(t, carry):
        stage_a(2 * t + 1, 1)
        stage_b(2 * t, 0, False)
        stage_a(2 * t + 2, 0)
        stage_b(2 * t + 1, 1, False)
        return carry

    lax.fori_loop(0, n_full // 2, pair, 0)

    @pl.when(n_full % 2 == 1)
    def _():
        stage_a(n_full, 1)
        stage_b(n_full - 1, 0, False)
        stage_b(n_full, 1, True)

    @pl.when(n_full % 2 == 0)
    def _():
        stage_b(n_full, 0, True)

    o = acc_scr[...] / l_scr[...]
    o_ref[...] = o.reshape(MLA_HEADS, tq, kv_lora).astype(o_ref.dtype)


def _attn_prompt(qcat, kcat, B, T, kv_lora, tq, tk):
    tk = min(tk, T)
    tq = min(tq, tk)
    assert tk % tq == 0 and T % tk == 0
    nq = T // tq
    kcw = kcat.shape[-1]
    rows = MLA_HEADS * tq
    return pl.pallas_call(
        functools.partial(_attn_prompt_kernel, tq, tk, kv_lora),
        grid=(B, nq),
        in_specs=[pl.BlockSpec((MLA_HEADS, tq, kcw), lambda b, i: (0, b * nq + i, 0)),
                  pl.BlockSpec((T, kcw), lambda b, i: (b, 0))],
        out_specs=pl.BlockSpec((MLA_HEADS, tq, kv_lora), lambda b, i: (0, b * nq + i, 0)),
        out_shape=jax.ShapeDtypeStruct((MLA_HEADS, B * T, kv_lora), BF16),
        scratch_shapes=[pltpu.VMEM((rows, tk), F32), pltpu.VMEM((rows, tk), F32),
                        pltpu.VMEM((rows, 1), F32), pltpu.VMEM((rows, 1), F32),
                        pltpu.VMEM((rows, kv_lora), F32)],
        compiler_params=_params(("parallel", "arbitrary")),
        name="attn_prompt",
    )(qcat, kcat)


def _attn_sample_kernel(gp, gps, ns, kv_lora, pt_ref, q_ref, cn_ref, krn_ref, ckv_hbm, kr_hbm,
                        o_ref, *scr):
    craw, kraw, rest = scr[0:ns], scr[ns:2 * ns], scr[2 * ns:]
    cbf, kbf, s_scr = rest[0:2], rest[2:4], rest[4:6]
    sem_c, sem_k, m_scr, l_scr, acc_scr = rest[6:]
    b = pl.program_id(0)
    nb = pl.num_programs(0)
    total = nb * gps
    base = b * gps
    t_new = q_ref.shape[1]
    rows = MLA_HEADS * t_new

    def copies(grp, slot):
        out = []
        for j in range(gp):
            pid = pt_ref[grp * gp + j]
            out.append(pltpu.make_async_copy(ckv_hbm.at[pid], craw[slot].at[j], sem_c.at[slot]))
            out.append(pltpu.make_async_copy(kr_hbm.at[pid], kraw[slot].at[j], sem_k.at[slot]))
        return out

    def start(grp, slot):
        for c in copies(grp, slot):
            c.start()

    def wait(grp, slot):
        for c in copies(grp, slot):
            c.wait()

    def start_ahead(g):
        nxt = base + g + ns
        start(jnp.where(nxt < total, nxt, nxt - total), g % ns)

    @pl.when(b == 0)
    def _():
        for g in range(ns):
            start(g, g)

    q = q_ref[...].reshape(rows, q_ref.shape[-1])
    qa = q[:, :kv_lora]
    qr = q[:, kv_lora:kv_lora + QK_ROPE]
    cn = cn_ref[...]
    s_new = _dot_nt(qa, cn) + _dot_nt(qr, krn_ref[...])
    q_t = lax.broadcasted_iota(jnp.int32, s_new.shape, 0) % t_new
    k_t = lax.broadcasted_iota(jnp.int32, s_new.shape, 1)
    s_new = jnp.where(k_t <= q_t, s_new, NEG_INF)
    m0 = jnp.max(s_new, axis=-1, keepdims=True)
    p0 = jnp.exp(s_new - m0)
    m_scr[...] = m0
    l_scr[...] = jnp.sum(p0, axis=-1, keepdims=True)
    acc_scr[...] = _dot(p0, cn)
    qa_b = qa.astype(BF16)
    qr_b = qr.astype(BF16)

    def stage_a(raw, slot):
        for j in range(gp):
            cbf[slot][j * PAGE_SIZE:(j + 1) * PAGE_SIZE, :] = craw[raw][j].astype(BF16)
            kbf[slot][:, j * PAGE_SIZE:(j + 1) * PAGE_SIZE] = kraw[raw][j].astype(BF16)
        s_scr[slot][...] = _dot_nt(qa_b, cbf[slot][...]) + _dot(qr_b, kbf[slot][...])

    def stage_b(slot):
        s = s_scr[slot][...]
        m_prev = m_scr[...]
        m_new = jnp.maximum(m_prev, jnp.max(s, axis=-1, keepdims=True))
        alpha = jnp.exp(m_prev - m_new)
        pr = jnp.exp(s - m_new)
        l_scr[...] = alpha * l_scr[...] + jnp.sum(pr, axis=-1, keepdims=True)
        acc_scr[...] = alpha * acc_scr[...] + _dot(pr.astype(BF16), cbf[slot][...])
        m_scr[...] = m_new

    wait(base, 0)
    stage_a(0, 0)
    for g in range(gps - 1):
        start_ahead(g)
        wait(base + g + 1, (g + 1) % ns)
        stage_a((g + 1) % ns, (g + 1) % 2)
        stage_b(g % 2)
    start_ahead(gps - 1)
    stage_b((gps - 1) % 2)

    o = acc_scr[...] / l_scr[...]
    o_ref[...] = o.reshape(MLA_HEADS, t_new, kv_lora)

    @pl.when(b == nb - 1)
    def _():
        for g in range(ns):
            wait(g, g)


def _attn_sample(qcat, c_new, kr_new, cache_ckv, cache_krope, page_table, gp, ns):
    B, n_pages = page_table.shape
    t_new = c_new.shape[0] // B
    kv_lora = c_new.shape[-1]
    kcw = qcat.shape[-1]
    ns = min(ns, n_pages)
    ns -= ns % 2
    gp = min(gp, n_pages // ns)
    while n_pages % (ns * gp):
        gp -= 1
    gps = n_pages // gp
    rows = MLA_HEADS * t_new
    keys = gp * PAGE_SIZE
    pt_flat = page_table.reshape(-1)
    krope_t = jnp.swapaxes(cache_krope, 1, 2)
    many = lambda n, shape, dt: [pltpu.VMEM(shape, dt) for _ in range(n)]
    grid_spec = pltpu.PrefetchScalarGridSpec(
        num_scalar_prefetch=1,
        grid=(B,),
        in_specs=[pl.BlockSpec((MLA_HEADS, t_new, kcw), lambda b, pt: (0, b, 0)),
                  pl.BlockSpec((t_new, kv_lora), lambda b, pt: (b, 0)),
                  pl.BlockSpec((t_new, QK_ROPE), lambda b, pt: (b, 0)),
                  pl.BlockSpec(memory_space=pl.ANY),
                  pl.BlockSpec(memory_space=pl.ANY)],
        out_specs=pl.BlockSpec((MLA_HEADS, t_new, kv_lora), lambda b, pt: (0, b, 0)),
        scratch_shapes=many(ns, (gp, PAGE_SIZE, kv_lora), F32)
                       + many(ns, (gp, QK_ROPE, PAGE_SIZE), F32)
                       + many(2, (keys, kv_lora), BF16) + many(2, (QK_ROPE, keys), BF16)
                       + many(2, (rows, keys), F32)
                       + [pltpu.SemaphoreType.DMA((ns,)), pltpu.SemaphoreType.DMA((ns,)),
                          pltpu.VMEM((rows, 1), F32), pltpu.VMEM((rows, 1), F32),
                          pltpu.VMEM((rows, kv_lora), F32)],
    )
    return pl.pallas_call(
        functools.partial(_attn_sample_kernel, gp, gps, ns, kv_lora),
        grid_spec=grid_spec,
        out_shape=jax.ShapeDtypeStruct((MLA_HEADS, B * t_new, kv_lora), F32),
        compiler_params=_params(("arbitrary",)),
        name="attn_sample",
    )(pt_flat, qcat, c_new, kr_new, cache_ckv, krope_t)


def _route_kernel(n_exp, ne8, h_ref, o_ref, wuv_ref, wo_ref, g_ref, wr_ref, tri_ref,
                  h3_ref, hn_ref, comb_ref, rtm_ref, rem_ref, cnt_ref):
    heads = [_dot(o_ref[hh].astype(BF16), wuv_ref[hh]).astype(BF16) for hh in range(MLA_HEADS)]
    h3 = h_ref[...] + _dot(jnp.concatenate(heads, axis=-1), wo_ref[...])
    h3_ref[...] = h3
    hn = _rms(h3, g_ref[...])
    hn_ref[...] = hn.astype(BF16)
    lg = jnp.dot(hn, wr_ref[...], preferred_element_type=F32, precision=lax.Precision.HIGHEST)
    lane = lax.broadcasted_iota(jnp.int32, lg.shape, 1)
    lg = jnp.where(lane < n_exp, lg, -jnp.inf)
    m1 = jnp.max(lg, axis=-1, keepdims=True)
    i1 = jnp.min(jnp.where(lg == m1, lane, LANE), axis=-1, keepdims=True)
    lg2 = jnp.where(lane == i1, -jnp.inf, lg)
    m2 = jnp.max(lg2, axis=-1, keepdims=True)
    i2 = jnp.min(jnp.where(lg2 == m2, lane, LANE), axis=-1, keepdims=True)
    t = jnp.exp(m2 - m1)
    g1 = 1.0 / (1.0 + t)
    comb_ref[...] = jnp.where(lane == i1, g1, 0.0) + jnp.where(lane == i2, t * g1, 0.0)
    routed = jnp.where(lane == i1, 1.0, 0.0) + jnp.where(lane == i2, 1.0, 0.0)
    rank = _dot(tri_ref[...], routed.astype(BF16))
    rtm = jnp.where(routed > 0.0, rank, -1.0)
    rtm_ref[...] = rtm
    rem_ref[...] = rtm.T[:ne8]
    cnt_ref[...] = jnp.max(rtm, axis=0, keepdims=True) + 1.0


def _route(h2, o_lat, p, sb):
    n, d = h2.shape
    n_exp = p["w_e_gate"].shape[0]
    ne8 = -(-n_exp // SUBLANE) * SUBLANE
    kv_lora = o_lat.shape[-1]
    nb = n // sb
    tri = jnp.asarray(np.tril(np.ones((sb, sb), np.float32), -1), BF16)
    c2 = lambda i: (0, 0)
    row = lambda w: pl.BlockSpec((sb, w), lambda i: (i, 0))
    return pl.pallas_call(
        functools.partial(_route_kernel, n_exp, ne8),
        grid=(nb,),
        in_specs=[row(d),
                  pl.BlockSpec((MLA_HEADS, sb, kv_lora), lambda i: (0, i, 0)),
                  pl.BlockSpec(p["w_uv"].shape, lambda i: (0, 0, 0)),
                  pl.BlockSpec((d, d), c2),
                  pl.BlockSpec((1, d), c2),
                  pl.BlockSpec((d, LANE), c2),
                  pl.BlockSpec((sb, sb), c2)],
        out_specs=[row(d), row(d), row(LANE), row(LANE),
                   pl.BlockSpec((None, ne8, sb), lambda i: (i, 0, 0)),
                   pl.BlockSpec((None, 1, LANE), lambda i: (i, 0, 0))],
        out_shape=[jax.ShapeDtypeStruct((n, d), F32),
                   jax.ShapeDtypeStruct((n, d), BF16),
                   jax.ShapeDtypeStruct((n, LANE), F32),
                   jax.ShapeDtypeStruct((n, LANE), F32),
                   jax.ShapeDtypeStruct((nb, ne8, sb), F32),
                   jax.ShapeDtypeStruct((nb, 1, LANE), F32)],
        compiler_params=_params(("parallel",)),
        name="route",
    )(h2, o_lat, p["w_uv"], p["w_out_b"], p["g_ffn1"], p["w_router"], tri)


def _moe_kernel(rt, tmax, cnt_ref, hn_ref, rem_ref, rtm_ref, comb_ref, h3_ref, gfin_ref,
                wg_ref, wu_ref, wd_ref, out_ref, xg_scr, y_scr):
    j = pl.program_id(0)
    e = pl.program_id(1)
    f = pl.program_id(2)
    ne = pl.num_programs(1)
    nf = pl.num_programs(2)
    sb = hn_ref.shape[0]
    n_t = (cnt_ref[j * ne + e] + rt - 1) // rt

    @pl.when((e == 0) & (f == 0))
    def _():
        out_ref[...] = h3_ref[...]

    @pl.when(f == 0)
    def _():
        rank_row = rem_ref[pl.ds(e, 1), :]
        for t in range(tmax):
            @pl.when(t < n_t)
            def _():
                r = (t * rt + lax.broadcasted_iota(jnp.int32, (rt, sb), 0)).astype(F32)
                onehot = jnp.where(rank_row == r, 1.0, 0.0).astype(BF16)
                xg_scr[t] = _dot(onehot, hn_ref[...]).astype(BF16)
                y_scr[t] = jnp.zeros((rt, y_scr.shape[-1]), F32)

    for t in range(tmax):
        @pl.when(t < n_t)
        def _():
            x = xg_scr[t]
            a = _dot(x, wg_ref[...])
            u = _dot(x, wu_ref[...])
            y_scr[t] += _dot((a * _sigmoid(a) * u).astype(BF16), wd_ref[...])

    @pl.when(f == nf - 1)
    def _():
        sel = lax.broadcasted_iota(jnp.int32, (sb, LANE), 1) == e
        rank_col = jnp.sum(jnp.where(sel, rtm_ref[...], 0.0), axis=-1, keepdims=True)
        gate_col = jnp.sum(jnp.where(sel, comb_ref[...], 0.0), axis=-1, keepdims=True)
        for t in range(tmax):
            @pl.when(t < n_t)
            def _():
                r = (t * rt + lax.broadcasted_iota(jnp.int32, (sb, rt), 1)).astype(F32)
                onehot_t = jnp.where(rank_col == r, 1.0, 0.0).astype(BF16)
                out_ref[...] += gate_col * _dot(onehot_t, y_scr[t].astype(BF16))

    @pl.when((e == ne - 1) & (f == nf - 1))
    def _():
        out_ref[...] = _rms(out_ref[...], gfin_ref[...])


def _moe(h3, hn, comb, rtm, rem, cnt, p, sb, rt):
    n, d = h3.shape
    n_exp, nf, _, tf = p["w_e_gate"].shape
    rt = min(rt, sb)
    tmax = -(-sb // rt)
    ne8 = rem.shape[1]
    once = dict(pipeline_mode=pl.Buffered(1))
    grid_spec = pltpu.PrefetchScalarGridSpec(
        num_scalar_prefetch=1,
        grid=(n // sb, n_exp, nf),
        in_specs=[pl.BlockSpec((sb, d), lambda j, e, f, c: (j, 0), **once),
                  pl.BlockSpec((None, ne8, sb), lambda j, e, f, c: (j, 0, 0), **once),
                  pl.BlockSpec((sb, LANE), lambda j, e, f, c: (j, 0), **once),
                  pl.BlockSpec((sb, LANE), lambda j, e, f, c: (j, 0), **once),
                  pl.BlockSpec((sb, d), lambda j, e, f, c: (j, 0), **once),
                  pl.BlockSpec((1, d), lambda j, e, f, c: (0, 0)),
                  pl.BlockSpec((None, None, d, tf), lambda j, e, f, c: (e, f, 0, 0)),
                  pl.BlockSpec((None, None, d, tf), lambda j, e, f, c: (e, f, 0, 0)),
                  pl.BlockSpec((None, tf, d), lambda j, e, f, c: (e, f, 0))],
        out_specs=pl.BlockSpec((sb, d), lambda j, e, f, c: (j, 0)),
        scratch_shapes=[pltpu.VMEM((tmax, rt, d), BF16), pltpu.VMEM((tmax, rt, d), F32)],
    )
    return pl.pallas_call(
        functools.partial(_moe_kernel, rt, tmax),
        grid_spec=grid_spec,
        out_shape=jax.ShapeDtypeStruct((n, d), F32),
        compiler_params=_params(("parallel", "arbitrary", "arbitrary")),
        name="moe",
    )(cnt, hn, rem, rtm, comb, h3, p["g_final"], p["w_e_gate"], p["w_e_up"], p["w_e_down"])


def _rope_tables(pos):
    half = QK_ROPE // 2
    inv = ROPE_THETA ** (-jnp.arange(half, dtype=F32) / half)
    ang = pos.astype(F32)[:, None] * inv[None, :]
    cos, sin = jnp.cos(ang), jnp.sin(ang)
    z = jnp.zeros((pos.shape[0], ROPE_PAD - QK_ROPE), F32)
    return (jnp.concatenate([cos, cos, z], axis=-1), jnp.concatenate([-sin, sin, z], axis=-1))


def _swap_halves(w):
    half = w.shape[-1] // 2
    return jnp.concatenate([w[..., half:], w[..., :half]], axis=-1)


def _pad_rope(w):
    return jnp.pad(w, [(0, 0)] * (w.ndim - 1) + [(0, ROPE_PAD - QK_ROPE)])


def _prepare(g_mix_a, w_in_a, gamma_lb, g_onorm_a, w_out_a, g_kv_in, w_dkv, g_kv, w_ukv,
             g_mix_b, w_dq, g_q, w_uq, w_out_b, g_ffn, w_ff_gate, w_ff_up, w_ff_down,
             w_router, w_e_gate, w_e_up, w_e_down, g_final):
    kv_lora = g_kv.shape[-1]
    q_lora = g_q.shape[-1]
    row = lambda g: g.reshape(1, -1)
    wk_rope = w_dkv[:, kv_lora:]
    w_uq3 = w_uq[0].reshape(q_lora, MLA_HEADS, QK_NOPE + QK_ROPE)
    wq_rope = w_uq3[..., QK_NOPE:]
    w_ukv3 = w_ukv.reshape(kv_lora, MLA_HEADS, QK_NOPE + V_DIM)
    n_exp = w_router.shape[-1]
    tf = _pick_tf(w_ff_gate.shape[-1], FF_CHUNK)
    tfe = _pick_tf(w_e_gate.shape[-1], FF_CHUNK)
    return {
        "g_mix_a": g_mix_a[0], "w_in": w_in_a[0].astype(BF16), "gamma_lb": gamma_lb,
        "g_onorm": g_onorm_a[0], "w_out_a": w_out_a[0].astype(BF16),
        "g_ffn0": g_ffn[0], "w_ff_gate": _chunk_cols(w_ff_gate[0].astype(BF16), tf),
        "w_ff_up": _chunk_cols(w_ff_up[0].astype(BF16), tf), "w_ff_down": w_ff_down[0].astype(BF16),
        "g_kv_in": row(g_kv_in), "g_kv": row(g_kv), "g_mix_b": row(g_mix_b[0]), "g_q": row(g_q[0]),
        "w_kv": jnp.concatenate([w_dkv[:, :kv_lora], _pad_rope(wk_rope)], axis=-1).astype(BF16),
        "w_kvs": _pad_rope(_swap_halves(wk_rope)).astype(BF16),
        "w_dq": w_dq[0].astype(BF16),
        "w_qn": w_uq3[..., :QK_NOPE].reshape(q_lora, -1).astype(BF16),
        "w_qr": _pad_rope(wq_rope).reshape(q_lora, -1).astype(BF16),
        "w_qrs": _pad_rope(_swap_halves(wq_rope)).reshape(q_lora, -1).astype(BF16),
        "w_uk": jnp.transpose(w_ukv3[..., :QK_NOPE], (1, 2, 0)).astype(BF16),
        "w_uv": jnp.transpose(w_ukv3[..., QK_NOPE:], (1, 0, 2)).astype(BF16),
        "w_out_b": w_out_b[0].astype(BF16),
        "g_ffn1": row(g_ffn[1]), "g_final": row(g_final),
        "w_router": jnp.pad(w_router[0], ((0, 0), (0, LANE - n_exp))),
        "w_e_gate": _chunk_cols(w_e_gate[0].astype(BF16), tfe),
        "w_e_up": _chunk_cols(w_e_up[0].astype(BF16), tfe),
        "w_e_down": w_e_down[0].astype(BF16),
    }


def _trunk(x, pos, s0, p, attend, q_dtype, gla_bb, tm):
    B, T, d = x.shape
    n = B * T
    x2 = x.reshape(n, d)
    act_dtype = BF16 if T % HGRN_CHUNK == 0 else F32
    q, k, lf, v, sg = _hgrn_proj(x2, p["g_mix_a"], p["gamma_lb"], p["w_in"], 0, tm, act_dtype)
    r3 = lambda a: a.reshape(B, T, d)
    og, s_t = _gla(r3(q), r3(k), r3(lf), r3(v), r3(sg), p["g_onorm"], s0, gla_bb, 512)
    h2 = _ffn(x2, og.reshape(n, d), p["w_out_a"], p["g_ffn0"], p["w_ff_gate"], p["w_ff_up"],
              p["w_ff_down"], tm)
    cos_t, sin_t = _rope_tables(pos)
    c, kr, kcat, qcat = _kvq(h2, cos_t, sin_t, p, q_dtype, tm)
    o_lat = attend(qcat, kcat, c, kr)
    sb = min(1024, n)
    h3, hn, comb, rtm, rem, cnt = _route(h2, o_lat, p, sb)
    n_exp = p["w_e_gate"].shape[0]
    cnt_i = cnt[:, 0, :n_exp].astype(jnp.int32).reshape(-1)
    y = _moe(h3, hn, comb, rtm, rem, cnt_i, p, sb, 288)
    kv_lora = c.shape[-1]
    return (y.reshape(B, T, d), c.reshape(B, T, kv_lora), kr.reshape(B, T, QK_ROPE), s_t[None])


def kernel(x_prompt, x_sample, cache_ckv, cache_krope, state_hgrn, page_table, g_mix_a, w_in_a,
           gamma_lb, g_onorm_a, w_out_a, g_kv_in, w_dkv, g_kv, w_ukv, g_mix_b, w_dq, g_q, w_uq,
           w_out_b, g_ffn, w_ff_gate, w_ff_up, w_ff_down, w_router, w_e_gate, w_e_up, w_e_down,
           g_final):
    p = _prepare(g_mix_a, w_in_a, gamma_lb, g_onorm_a, w_out_a, g_kv_in, w_dkv, g_kv, w_ukv,
                 g_mix_b, w_dq, g_q, w_uq, w_out_b, g_ffn, w_ff_gate, w_ff_up, w_ff_down,
                 w_router, w_e_gate, w_e_up, w_e_down, g_final)
    kv_lora = g_kv.shape[-1]
    bp, tp, _ = x_prompt.shape
    bs, ts, _ = x_sample.shape

    def attend_p(qcat, kcat, c, kr):
        return _attn_prompt(qcat, kcat, bp, tp, kv_lora, 128, 512)

    y_p, c_p, kr_p, s_p = _trunk(x_prompt, jnp.arange(tp, dtype=F32), None, p, attend_p,
                                 BF16, 1, 512)

    past = page_table.shape[1] * PAGE_SIZE
    pos_s = jnp.tile(past + jnp.arange(ts, dtype=F32), bs)

    def attend_s(qcat, kcat, c, kr):
        return _attn_sample(qcat, c, kr, cache_ckv, cache_krope, page_table, 8, 8)

    y_s, c_s, kr_s, s_s = _trunk(x_sample, pos_s, state_hgrn[0], p, attend_s, F32, 8, 512)
    return (y_p, y_s, c_p, kr_p, c_s, kr_s, s_p.astype(state_hgrn.dtype),
            s_s.astype(state_hgrn.dtype))
```

```python
import functools

import numpy as np
import jax
import jax.numpy as jnp
from jax import lax
from jax.experimental import pallas as pl
from jax.experimental.pallas import tpu as pltpu

F32 = jnp.float32
BF16 = jnp.bfloat16

RMS_EPS = 1e-6
ROPE_THETA = 10000.0
HGRN_HEADS = 8
HGRN_DK = 128
HGRN_CHUNK = 128
MLA_HEADS = 8
QK_NOPE = 128
QK_ROPE = 64
V_DIM = 128
PAGE_SIZE = 128
TOP_K = 2
SM_SCALE = (QK_NOPE + QK_ROPE) ** -0.5
NEG_INF = -1e30
LANE = 128
SUBLANE = 8
ROPE_PAD = LANE
VMEM_LIMIT = 56 * 1024 * 1024
FF_CHUNK = 1408
FF_COLS = 512


def _params(sem):
    return pltpu.CompilerParams(dimension_semantics=sem, vmem_limit_bytes=VMEM_LIMIT)


def _dot(a, b):
    return jnp.dot(a, b, preferred_element_type=F32)


def _dot_nt(a, b):
    return lax.dot_general(a, b, (((1,), (1,)), ((), ())), preferred_element_type=F32)


def _dot_tn(a, b):
    return lax.dot_general(a, b, (((0,), (0,)), ((), ())), preferred_element_type=F32)


def _rms(x, g):
    return x * lax.rsqrt(jnp.mean(x * x, axis=-1, keepdims=True) + RMS_EPS) * g


def _sigmoid(x):
    return 1.0 / (1.0 + jnp.exp(-x))


def _lane_fold(x, op):
    w = x.shape[-1]
    if w % LANE or w == LANE:
        return x
    acc = x[:, :LANE]
    for j in range(1, w // LANE):
        acc = op(acc, x[:, j * LANE:(j + 1) * LANE])
    return acc


def _hgrn_proj_kernel(layer, x_ref, g_ref, gam_ref, w_ref, q_ref, k_ref, lf_ref, i_ref, sg_ref):
    d = x_ref.shape[-1]
    xb = _rms(x_ref[...], g_ref[...]).astype(BF16)

    def proj(j):
        return _dot(xb, w_ref[:, j * d:(j + 1) * d])

    gam = gam_ref[...]
    e = jnp.exp(gam - jnp.max(gam, axis=0, keepdims=True))
    lb = jnp.sum(e[:layer + 1], axis=0, keepdims=True) / jnp.sum(e, axis=0, keepdims=True)

    q = proj(0)
    q_ref[...] = (q * _sigmoid(q)).astype(q_ref.dtype)
    z = proj(1)
    t = jnp.exp(-jnp.abs(z))
    r = 1.0 / (1.0 + t)
    pos = z >= 0
    sig_z = jnp.where(pos, r, t * r)
    sig_mz = jnp.where(pos, t * r, r)
    k_ref[...] = ((1.0 - lb) * sig_mz).astype(k_ref.dtype)
    lf_ref[...] = jnp.log(lb + (1.0 - lb) * sig_z)
    i_ref[...] = proj(2).astype(i_ref.dtype)
    g = proj(3)
    sg_ref[...] = (g * _sigmoid(g)).astype(sg_ref.dtype)


def _hgrn_proj(x2, g_mix, gamma_lb, w_in_bf, layer, tm, act_dtype):
    n, d = x2.shape
    tm = min(tm, n)
    row = pl.BlockSpec((tm, d), lambda i: (i, 0))
    out = jax.ShapeDtypeStruct((n, d), act_dtype)
    return pl.pallas_call(
        functools.partial(_hgrn_proj_kernel, layer),
        grid=(n // tm,),
        in_specs=[row,
                  pl.BlockSpec((1, d), lambda i: (0, 0)),
                  pl.BlockSpec(gamma_lb.shape, lambda i: (0, 0)),
                  pl.BlockSpec(w_in_bf.shape, lambda i: (0, 0))],
        out_specs=[row] * 5,
        out_shape=[out, out, jax.ShapeDtypeStruct((n, d), F32), out, out],
        compiler_params=_params(("parallel",)),
        name="hgrn_proj",
    )(x2, g_mix.reshape(1, d), gamma_lb, w_in_bf)


def _gla_constants(L):
    nl = int(np.log2(L))
    assert 2 ** nl == L
    t = np.arange(L)
    me = np.zeros((1 + nl, L, L), np.float32)
    me[0] = (t[None, :] <= t[:, None])
    masks = np.zeros((1 + nl, L, L), np.float32)
    masks[0] = np.eye(L)
    for l in range(nl):
        m = L >> (l + 1)
        blk = t // (2 * m)
        r = blk * 2 * m + m
        upper = t >= r
        j = t[None, :]
        up_rows = (j >= r[:, None]) & (j <= t[:, None])
        lo_rows = (j > t[:, None]) & (j <= r[:, None] - 1)
        me[1 + l] = np.where(upper[:, None], up_rows, lo_rows)
        masks[1 + l] = (upper[:, None] & ~upper[None, :] & (blk[:, None] == blk[None, :]))
    return me.reshape((1 + nl) * L, L), masks, nl


def _gla_kernel(L, nl, nh, has_s0, mm, *refs):
    s_scrs = refs[-nh:]
    refs = refs[:-nh]
    if has_s0:
        (q_ref, k_ref, lf_ref, v_ref, sg_ref, go_ref, me_ref, mk_ref, s0_ref,
         og_ref, st_ref) = refs
    else:
        (q_ref, k_ref, lf_ref, v_ref, sg_ref, go_ref, me_ref, mk_ref,
         og_ref, st_ref) = refs
        s0_ref = None
    bb, tb, _ = q_ref.shape
    nchunk = tb // L
    ti = pl.program_id(1)
    dk = HGRN_DK

    @pl.when(ti == 0)
    def _():
        if has_s0:
            def init(b, carry):
                for h in range(nh):
                    s_scrs[h][b] = s0_ref[b, h].T
                return carry
            lax.fori_loop(0, bb, init, 0)
        else:
            for h in range(nh):
                s_scrs[h][...] = jnp.zeros_like(s_scrs[h])

    me = me_ref[...]
    go = go_ref[...]

    def chunk(b, c):
        sl = pl.ds(pl.multiple_of(c * L, L), L)
        g = lf_ref[b, sl, :]
        if mm == BF16:
            g1 = g.astype(BF16)
            g2 = (g - g1.astype(F32)).astype(BF16)
            ee = _dot(me, g1) + _dot(me, g2)
        else:
            ee = _dot(me, g)
        for h in range(nh):
            hs = slice(h * dk, (h + 1) * dk)
            q = q_ref[b, sl, hs].astype(F32)
            k = k_ref[b, sl, hs].astype(F32)
            v = v_ref[b, sl, hs].astype(mm)
            st = s_scrs[h][b]
            b_cum = ee[0:L, hs]
            b_last = b_cum[L - 1:L, :]
            o = _dot_nt((q * jnp.exp(b_cum)).astype(mm), st.astype(mm))
            sc = _dot_nt(q.astype(mm), k.astype(mm)) * mk_ref[0]
            for l in range(nl):
                e = jnp.exp(ee[(1 + l) * L:(2 + l) * L, hs])
                sc = sc + _dot_nt((q * e).astype(mm), (k * e).astype(mm)) * mk_ref[1 + l]
            o = o + _dot(sc.astype(mm), v)
            kd = (k * jnp.exp(b_last - b_cum)).astype(mm)
            s_scrs[h][b] = st * jnp.exp(b_last) + _dot_tn(v, kd)
            og = _rms(o, go) * sg_ref[b, sl, hs].astype(F32)
            og_ref[b, sl, hs] = og.astype(og_ref.dtype)

    def body(i, carry):
        chunk(i // nchunk, i % nchunk)
        return carry

    lax.fori_loop(0, bb * nchunk, body, 0)

    @pl.when(ti == pl.num_programs(1) - 1)
    def _():
        def fin(b, carry):
            for h in range(nh):
                st_ref[b, h] = s_scrs[h][b].T
            return carry
        lax.fori_loop(0, bb, fin, 0)


def _gla(q, k, lf, v, sg, g_onorm, s0, bb, tb):
    B, T, hd = q.shape
    H = hd // HGRN_DK
    L = HGRN_CHUNK if T % HGRN_CHUNK == 0 else T
    tb = min(tb, T)
    bb = min(bb, B)
    mm = BF16 if L % 16 == 0 else F32
    me_np, mk_np, nl = _gla_constants(L)
    me = jnp.asarray(me_np, mm)
    mk = jnp.asarray(mk_np, F32)
    seq = pl.BlockSpec((bb, tb, hd), lambda b, t: (b, t, 0))
    st_spec = pl.BlockSpec((bb, H, HGRN_DK, HGRN_DK), lambda b, t: (b, 0, 0, 0))
    in_specs = [seq] * 5 + [
        pl.BlockSpec((1, HGRN_DK), lambda b, t: (0, 0)),
        pl.BlockSpec(me.shape, lambda b, t: (0, 0)),
        pl.BlockSpec(mk.shape, lambda b, t: (0, 0, 0)),
    ]
    args = [q, k, lf, v, sg, g_onorm.reshape(1, HGRN_DK), me, mk]
    if s0 is not None:
        in_specs.append(st_spec)
        args.append(s0)
    return pl.pallas_call(
        functools.partial(_gla_kernel, L, nl, H, s0 is not None, mm),
        grid=(B // bb, T // tb),
        in_specs=in_specs,
        out_specs=[seq, st_spec],
        out_shape=[jax.ShapeDtypeStruct((B, T, hd), BF16),
                   jax.ShapeDtypeStruct((B, H, HGRN_DK, HGRN_DK), F32)],
        scratch_shapes=[pltpu.VMEM((bb, HGRN_DK, HGRN_DK), F32) for _ in range(H)],
        compiler_params=_params(("parallel", "arbitrary")),
        name="gla",
    )(*args)


def _col_chunks(width, target):
    step = max(LANE, target // LANE * LANE)
    return [(c, min(c + step, width)) for c in range(0, width, step)]


def _swiglu_chunks(x, wg_ref, wu_ref, wd_ref, chunks):
    y = None
    for c0, c1 in chunks:
        a = _dot(x, wg_ref[:, c0:c1])
        u = _dot(x, wu_ref[:, c0:c1])
        part = _dot((a * _sigmoid(a) * u).astype(BF16), wd_ref[c0:c1, :])
        y = part if y is None else y + part
    return y


def _ffn_kernel(chunks, h_ref, og_ref, wo_ref, g_ref, wg_ref, wu_ref, wd_ref, out_ref, hn_scr):
    f = pl.program_id(1)

    @pl.when(f == 0)
    def _():
        h1 = h_ref[...] + _dot(og_ref[...], wo_ref[...])
        out_ref[...] = h1
        hn_scr[...] = _rms(h1, g_ref[...]).astype(BF16)

    out_ref[...] += _swiglu_chunks(hn_scr[...], wg_ref, wu_ref, wd_ref, chunks)


def _pick_tf(dff, target):
    best = None
    for tf in range(LANE, dff + 1, LANE):
        if dff % tf == 0 and tf <= target:
            best = tf
    return best or dff


def _ffn(h, og, w_out_bf, g_ffn, wg_bf, wu_bf, wd_bf, tm):
    n, d = h.shape
    dff = wg_bf.shape[1]
    tm = min(tm, n)
    tf = _pick_tf(dff, FF_CHUNK)
    row = lambda i, f: (i, 0)
    return pl.pallas_call(
        functools.partial(_ffn_kernel, _col_chunks(tf, FF_COLS)),
        grid=(n // tm, dff // tf),
        in_specs=[pl.BlockSpec((tm, d), row),
                  pl.BlockSpec((tm, d), row),
                  pl.BlockSpec((d, d), lambda i, f: (0, 0), pipeline_mode=pl.Buffered(1)),
                  pl.BlockSpec((1, d), lambda i, f: (0, 0)),
                  pl.BlockSpec((d, tf), lambda i, f: (0, f)),
                  pl.BlockSpec((d, tf), lambda i, f: (0, f)),
                  pl.BlockSpec((tf, d), lambda i, f: (f, 0))],
        out_specs=pl.BlockSpec((tm, d), row),
        out_shape=jax.ShapeDtypeStruct((n, d), F32),
        scratch_shapes=[pltpu.VMEM((tm, d), BF16)],
        compiler_params=_params(("parallel", "arbitrary")),
        name="ffn",
    )(h, og, w_out_bf, g_ffn.reshape(1, d), wg_bf, wu_bf, wd_bf)


def _kvq_kernel(kv_lora, h_ref, cos_ref, sin_ref, gkin_ref, wkv_ref, wkvs_ref, gkv_ref,
                gmix_ref, wdq_ref, gq_ref, wqn_ref, wqr_ref, wqrs_ref, wuk_ref,
                c_ref, kr_ref, kcat_ref, qcat_ref):
    h = h_ref[...]
    cos = cos_ref[...]
    sin = sin_ref[...]
    nb = _rms(h, gkin_ref[...]).astype(BF16)
    kv = _dot(nb, wkv_ref[...])
    kvs = _dot(nb, wkvs_ref[...])
    c = _rms(kv[:, :kv_lora], gkv_ref[...])
    kr = kv[:, kv_lora:] * cos + kvs * sin
    c_ref[...] = c
    kr_ref[...] = kr[:, :QK_ROPE]
    kcat_ref[:, :kv_lora] = c.astype(BF16)
    kcat_ref[:, kv_lora:] = kr.astype(BF16)
    xb = _rms(h, gmix_ref[...]).astype(BF16)
    cq = _rms(_dot(xb, wdq_ref[...]), gq_ref[...]).astype(BF16)
    qn = _dot(cq, wqn_ref[...])
    qr = _dot(cq, wqr_ref[...])
    qrs = _dot(cq, wqrs_ref[...])
    for hh in range(MLA_HEADS):
        qn_h = qn[:, hh * QK_NOPE:(hh + 1) * QK_NOPE].astype(BF16)
        qa_h = _dot(qn_h, wuk_ref[hh]) * SM_SCALE
        sl = slice(hh * ROPE_PAD, (hh + 1) * ROPE_PAD)
        qr_h = (qr[:, sl] * cos + qrs[:, sl] * sin) * SM_SCALE
        qcat_ref[hh, :, :kv_lora] = qa_h.astype(qcat_ref.dtype)
        qcat_ref[hh, :, kv_lora:] = qr_h.astype(qcat_ref.dtype)


def _kvq(h2, cos_t, sin_t, p, q_dtype, tm):
    n, d = h2.shape
    tm = min(tm, cos_t.shape[0])
    kv_lora = p["g_kv"].shape[-1]
    kcw = kv_lora + ROPE_PAD
    nt = cos_t.shape[0] // tm
    const = lambda a: pl.BlockSpec(a.shape, lambda i: (0,) * a.ndim)
    row = lambda w: pl.BlockSpec((tm, w), lambda i: (i, 0))
    tab = pl.BlockSpec((tm, ROPE_PAD), lambda i: (i % nt, 0))
    weights = [p["g_kv_in"], p["w_kv"], p["w_kvs"], p["g_kv"], p["g_mix_b"], p["w_dq"],
               p["g_q"], p["w_qn"], p["w_qr"], p["w_qrs"], p["w_uk"]]
    return pl.pallas_call(
        functools.partial(_kvq_kernel, kv_lora),
        grid=(n // tm,),
        in_specs=[row(d), tab, tab] + [const(w) for w in weights],
        out_specs=[row(kv_lora), row(QK_ROPE), row(kcw),
                   pl.BlockSpec((MLA_HEADS, tm, kcw), lambda i: (0, i, 0))],
        out_shape=[jax.ShapeDtypeStruct((n, kv_lora), F32),
                   jax.ShapeDtypeStruct((n, QK_ROPE), F32),
                   jax.ShapeDtypeStruct((n, kcw), BF16),
                   jax.ShapeDtypeStruct((MLA_HEADS, n, kcw), q_dtype)],
        compiler_params=_params(("parallel",)),
        name="kvq",
    )(h2, cos_t, sin_t, *weights)


def _attn_prompt_kernel(tq, tk, kv_lora, q_ref, k_ref, o_ref, s0_scr, s1_scr, m_scr, l_scr,
                        acc_scr):
    i = pl.program_id(1)
    rows = MLA_HEADS * tq
    s_scrs = (s0_scr, s1_scr)
    n_full = (i * tq) // tk

    m_scr[...] = jnp.full_like(m_scr, NEG_INF)
    l_scr[...] = jnp.zeros_like(l_scr)
    acc_scr[...] = jnp.zeros_like(acc_scr)

    def k_tile(j):
        return k_ref[pl.ds(pl.multiple_of(j * tk, tk), tk), :]

    def stage_a(j, slot):
        q = q_ref[...].reshape(rows, q_ref.shape[-1])
        s_scrs[slot][...] = _dot_nt(q, k_tile(j))

    def stage_b(j, slot, masked):
        s = s_scrs[slot][...]
        if masked:
            q_pos = i * tq + lax.broadcasted_iota(jnp.int32, (rows, tk), 0) % tq
            k_pos = j * tk + lax.broadcasted_iota(jnp.int32, (rows, tk), 1)
            s = jnp.where(k_pos <= q_pos, s, NEG_INF)
        m_prev = m_scr[...]
        m_new = jnp.maximum(m_prev, jnp.max(s, axis=-1, keepdims=True))
        alpha = jnp.exp(m_prev - m_new)
        pr = jnp.exp(s - m_new)
        l_scr[...] = alpha * l_scr[...] + jnp.sum(pr, axis=-1, keepdims=True)
        acc_scr[...] = alpha * acc_scr[...] + _dot(pr.astype(BF16), k_tile(j)[:, :kv_lora])
        m_scr[...] = m_new

    stage_a(0, 0)

    def pair(t, carry):
        stage_a(2 * t + 1, 1)
        stage_b(2 * t, 0, False)
        stage_a(2 * t + 2, 0)
        stage_b(2 * t + 1, 1, False)
        return carry

    lax.fori_loop(0, n_full // 2, pair, 0)

    @pl.when(n_full % 2 == 1)
    def _():
        stage_a(n_full, 1)
        stage_b(n_full - 1, 0, False)
        stage_b(n_full, 1, True)

    @pl.when(n_full % 2 == 0)
    def _():
        stage_b(n_full, 0, True)

    o = acc_scr[...] / l_scr[...]
    o_ref[...] = o.reshape(MLA_HEADS, tq, kv_lora).astype(o_ref.dtype)


def _attn_prompt(qcat, kcat, B, T, kv_lora, tq, tk):
    tk = min(tk, T)
    tq = min(tq, tk)
    assert tk % tq == 0 and T % tk == 0
    nq = T // tq
    kcw = kcat.shape[-1]
    rows = MLA_HEADS * tq
    return pl.pallas_call(
        functools.partial(_attn_prompt_kernel, tq, tk, kv_lora),
        grid=(B, nq),
        in_specs=[pl.BlockSpec((MLA_HEADS, tq, kcw), lambda b, i: (0, b * nq + i, 0)),
                  pl.BlockSpec((T, kcw), lambda b, i: (b, 0))],
        out_specs=pl.BlockSpec((MLA_HEADS, tq, kv_lora), lambda b, i: (0, b * nq + i, 0)),
        out_shape=jax.ShapeDtypeStruct((MLA_HEADS, B * T, kv_lora), BF16),
        scratch_shapes=[pltpu.VMEM((rows, tk), F32), pltpu.VMEM((rows, tk), F32),
                        pltpu.VMEM((rows, 1), F32), pltpu.VMEM((rows, 1), F32),
                        pltpu.VMEM((rows, kv_lora), F32)],
        compiler_params=_params(("parallel", "arbitrary")),
        name="attn_prompt",
    )(qcat, kcat)


def _attn_sample_kernel(gp, gps, ns, kv_lora, pt_ref, q_ref, cn_ref, krn_ref, ckv_hbm, kr_hbm,
                        o_ref, *scr):
    craw, kraw, rest = scr[0:ns], scr[ns:2 * ns], scr[2 * ns:]
    cbf, kbf, s_scr = rest[0:2], rest[2:4], rest[4:6]
    sem_c, sem_k, m_scr, l_scr, acc_scr = rest[6:]
    b = pl.program_id(0)
    nb = pl.num_programs(0)
    total = nb * gps
    base = b * gps
    t_new = q_ref.shape[1]
    rows = MLA_HEADS * t_new

    def copies(grp, slot):
        out = []
        for j in range(gp):
            pid = pt_ref[grp * gp + j]
            out.append(pltpu.make_async_copy(ckv_hbm.at[pid], craw[slot].at[j], sem_c.at[slot]))
            out.append(pltpu.make_async_copy(kr_hbm.at[pid], kraw[slot].at[j], sem_k.at[slot]))
        return out

    def start(grp, slot):
        for c in copies(grp, slot):
            c.start()

    def wait(grp, slot):
        for c in copies(grp, slot):
            c.wait()

    def start_ahead(g):
        nxt = base + g + ns
        start(jnp.where(nxt < total, nxt, nxt - total), g % ns)

    @pl.when(b == 0)
    def _():
        for g in range(ns):
            start(g, g)

    q = q_ref[...].reshape(rows, q_ref.shape[-1])
    qa = q[:, :kv_lora]
    qr = q[:, kv_lora:kv_lora + QK_ROPE]
    cn = cn_ref[...]
    s_new = _dot_nt(qa, cn) + _dot_nt(qr, krn_ref[...])
    q_t = lax.broadcasted_iota(jnp.int32, s_new.shape, 0) % t_new
    k_t = lax.broadcasted_iota(jnp.int32, s_new.shape, 1)
    s_new = jnp.where(k_t <= q_t, s_new, NEG_INF)
    m0 = jnp.max(s_new, axis=-1, keepdims=True)
    p0 = jnp.exp(s_new - m0)
    m_scr[...] = m0
    l_scr[...] = jnp.sum(p0, axis=-1, keepdims=True)
    acc_scr[...] = _dot(p0, cn)
    qa_b = qa.astype(BF16)
    qr_b = qr.astype(BF16)

    def stage_a(raw, slot):
        for j in range(gp):
            cbf[slot][j * PAGE_SIZE:(j + 1) * PAGE_SIZE, :] = craw[raw][j].astype(BF16)
            kbf[slot][:, j * PAGE_SIZE:(j + 1) * PAGE_SIZE] = kraw[raw][j].astype(BF16)
        s_scr[slot][...] = _dot_nt(qa_b, cbf[slot][...]) + _dot(qr_b, kbf[slot][...])

    def stage_b(slot):
        s = s_scr[slot][...]
        m_prev = m_scr[...]
        m_new = jnp.maximum(m_prev, jnp.max(s, axis=-1, keepdims=True))
        alpha = jnp.exp(m_prev - m_new)
        pr = jnp.exp(s - m_new)
        l_scr[...] = alpha * l_scr[...] + jnp.sum(pr, axis=-1, keepdims=True)
        acc_scr[...] = alpha * acc_scr[...] + _dot(pr.astype(BF16), cbf[slot][...])
        m_scr[...] = m_new

    wait(base, 0)
    stage_a(0, 0)
    for g in range(gps - 1):
        start_ahead(g)
        wait(base + g + 1, (g + 1) % ns)
        stage_a((g + 1) % ns, (g + 1) % 2)
        stage_b(g % 2)
    start_ahead(gps - 1)
    stage_b((gps - 1) % 2)

    o = acc_scr[...] / l_scr[...]
    o_ref[...] = o.reshape(MLA_HEADS, t_new, kv_lora)

    @pl.when(b == nb - 1)
    def _():
        for g in range(ns):
            wait(g, g)


def _attn_sample(qcat, c_new, kr_new, cache_ckv, cache_krope, page_table, gp, ns):
    B, n_pages = page_table.shape
    t_new = c_new.shape[0] // B
    kv_lora = c_new.shape[-1]
    kcw = qcat.shape[-1]
    ns = min(ns, n_pages)
    ns -= ns % 2
    gp = min(gp, n_pages // ns)
    while n_pages % (ns * gp):
        gp -= 1
    gps = n_pages // gp
    rows = MLA_HEADS * t_new
    keys = gp * PAGE_SIZE
    pt_flat = page_table.reshape(-1)
    krope_t = jnp.swapaxes(cache_krope, 1, 2)
    many = lambda n, shape, dt: [pltpu.VMEM(shape, dt) for _ in range(n)]
    grid_spec = pltpu.PrefetchScalarGridSpec(
        num_scalar_prefetch=1,
        grid=(B,),
        in_specs=[pl.BlockSpec((MLA_HEADS, t_new, kcw), lambda b, pt: (0, b, 0)),
                  pl.BlockSpec((t_new, kv_lora), lambda b, pt: (b, 0)),
                  pl.BlockSpec((t_new, QK_ROPE), lambda b, pt: (b, 0)),
                  pl.BlockSpec(memory_space=pl.ANY),
                  pl.BlockSpec(memory_space=pl.ANY)],
        out_specs=pl.BlockSpec((MLA_HEADS, t_new, kv_lora), lambda b, pt: (0, b, 0)),
        scratch_shapes=many(ns, (gp, PAGE_SIZE, kv_lora), F32)
                       + many(ns, (gp, QK_ROPE, PAGE_SIZE), F32)
                       + many(2, (keys, kv_lora), BF16) + many(2, (QK_ROPE, keys), BF16)
                       + many(2, (rows, keys), F32)
                       + [pltpu.SemaphoreType.DMA((ns,)), pltpu.SemaphoreType.DMA((ns,)),
                          pltpu.VMEM((rows, 1), F32), pltpu.VMEM((rows, 1), F32),
                          pltpu.VMEM((rows, kv_lora), F32)],
    )
    return pl.pallas_call(
        functools.partial(_attn_sample_kernel, gp, gps, ns, kv_lora),
        grid_spec=grid_spec,
        out_shape=jax.ShapeDtypeStruct((MLA_HEADS, B * t_new, kv_lora), F32),
        compiler_params=_params(("arbitrary",)),
        name="attn_sample",
    )(pt_flat, qcat, c_new, kr_new, cache_ckv, krope_t)


def _route_kernel(n_exp, ne8, h_ref, o_ref, wuv_ref, wo_ref, g_ref, wr_ref, tri_ref,
                  h3_ref, hn_ref, comb_ref, rtm_ref, rem_ref, cnt_ref):
    heads = [_dot(o_ref[hh].astype(BF16), wuv_ref[hh]).astype(BF16) for hh in range(MLA_HEADS)]
    h3 = h_ref[...] + _dot(jnp.concatenate(heads, axis=-1), wo_ref[...])
    h3_ref[...] = h3
    hn = _rms(h3, g_ref[...])
    hn_ref[...] = hn.astype(BF16)
    lg = jnp.dot(hn, wr_ref[...], preferred_element_type=F32, precision=lax.Precision.HIGHEST)
    lane = lax.broadcasted_iota(jnp.int32, lg.shape, 1)
    lg = jnp.where(lane < n_exp, lg, -jnp.inf)
    m1 = jnp.max(lg, axis=-1, keepdims=True)
    i1 = jnp.min(jnp.where(lg == m1, lane, LANE), axis=-1, keepdims=True)
    lg2 = jnp.where(lane == i1, -jnp.inf, lg)
    m2 = jnp.max(lg2, axis=-1, keepdims=True)
    i2 = jnp.min(jnp.where(lg2 == m2, lane, LANE), axis=-1, keepdims=True)
    t = jnp.exp(m2 - m1)
    g1 = 1.0 / (1.0 + t)
    comb_ref[...] = jnp.where(lane == i1, g1, 0.0) + jnp.where(lane == i2, t * g1, 0.0)
    routed = jnp.where(lane == i1, 1.0, 0.0) + jnp.where(lane == i2, 1.0, 0.0)
    rank = _dot(tri_ref[...], routed.astype(BF16))
    rtm = jnp.where(routed > 0.0, rank, -1.0)
    rtm_ref[...] = rtm
    rem_ref[...] = rtm.T[:ne8]
    cnt_ref[...] = jnp.max(rtm, axis=0, keepdims=True) + 1.0


def _route(h2, o_lat, p, sb):
    n, d = h2.shape
    n_exp = p["w_e_gate"].shape[0]
    ne8 = -(-n_exp // SUBLANE) * SUBLANE
    kv_lora = o_lat.shape[-1]
    nb = n // sb
    tri = jnp.asarray(np.tril(np.ones((sb, sb), np.float32), -1), BF16)
    c2 = lambda i: (0, 0)
    row = lambda w: pl.BlockSpec((sb, w), lambda i: (i, 0))
    return pl.pallas_call(
        functools.partial(_route_kernel, n_exp, ne8),
        grid=(nb,),
        in_specs=[row(d),
                  pl.BlockSpec((MLA_HEADS, sb, kv_lora), lambda i: (0, i, 0)),
                  pl.BlockSpec(p["w_uv"].shape, lambda i: (0, 0, 0)),
                  pl.BlockSpec((d, d), c2),
                  pl.BlockSpec((1, d), c2),
                  pl.BlockSpec((d, LANE), c2),
                  pl.BlockSpec((sb, sb), c2)],
        out_specs=[row(d), row(d), row(LANE), row(LANE),
                   pl.BlockSpec((None, ne8, sb), lambda i: (i, 0, 0)),
                   pl.BlockSpec((None, 1, LANE), lambda i: (i, 0, 0))],
        out_shape=[jax.ShapeDtypeStruct((n, d), F32),
                   jax.ShapeDtypeStruct((n, d), BF16),
                   jax.ShapeDtypeStruct((n, LANE), F32),
                   jax.ShapeDtypeStruct((n, LANE), F32),
                   jax.ShapeDtypeStruct((nb, ne8, sb), F32),
                   jax.ShapeDtypeStruct((nb, 1, LANE), F32)],
        compiler_params=_params(("parallel",)),
        name="route",
    )(h2, o_lat, p["w_uv"], p["w_out_b"], p["g_ffn1"], p["w_router"], tri)


def _moe_kernel(rt, tmax, e, n_exp, last, chunks, cnt_ref, hn_ref, rem_ref, rtm_ref, comb_ref,
                acc_ref, gfin_ref, wg_ref, wu_ref, wd_ref, out_ref):
    j = pl.program_id(0)
    sb = hn_ref.shape[0]
    n_t = (cnt_ref[j * n_exp + e] + rt - 1) // rt
    out_ref[...] = acc_ref[...]
    rank_row = rem_ref[e:e + 1, :]
    sel = lax.broadcasted_iota(jnp.int32, (sb, LANE), 1) == e
    rank_col = jnp.sum(jnp.where(sel, rtm_ref[...], 0.0), axis=-1, keepdims=True)
    gate_col = jnp.sum(jnp.where(sel, comb_ref[...], 0.0), axis=-1, keepdims=True)
    for t in range(tmax):
        @pl.when(t < n_t)
        def _():
            r = (t * rt + lax.broadcasted_iota(jnp.int32, (rt, sb), 0)).astype(F32)
            onehot = jnp.where(rank_row == r, 1.0, 0.0).astype(BF16)
            x = _dot(onehot, hn_ref[...]).astype(BF16)
            y = _swiglu_chunks(x, wg_ref, wu_ref, wd_ref, chunks)
            rc = (t * rt + lax.broadcasted_iota(jnp.int32, (sb, rt), 1)).astype(F32)
            onehot_t = jnp.where(rank_col == rc, 1.0, 0.0).astype(BF16)
            out_ref[...] += gate_col * _dot(onehot_t, y.astype(BF16))
    if last:
        out_ref[...] = _rms(out_ref[...], gfin_ref[...])


def _moe(h3, hn, comb, rtm, rem, cnt, p, sb, rt):
    n, d = h3.shape
    n_exp, _, dff = p["w_e_gate"].shape
    rt = min(rt, sb)
    tmax = -(-sb // rt)
    ne8 = rem.shape[1]
    blk = lambda w: pl.BlockSpec((sb, w), lambda j, c: (j, 0))
    acc = h3
    for e in range(n_exp):
        expert = lambda j, c, e=e: (e, 0, 0)
        grid_spec = pltpu.PrefetchScalarGridSpec(
            num_scalar_prefetch=1,
            grid=(n // sb,),
            in_specs=[blk(d),
                      pl.BlockSpec((None, ne8, sb), lambda j, c: (j, 0, 0)),
                      blk(LANE), blk(LANE), blk(d),
                      pl.BlockSpec((1, d), lambda j, c: (0, 0)),
                      pl.BlockSpec((None, d, dff), expert, pipeline_mode=pl.Buffered(1)),
                      pl.BlockSpec((None, d, dff), expert, pipeline_mode=pl.Buffered(1)),
                      pl.BlockSpec((None, dff, d), expert, pipeline_mode=pl.Buffered(1))],
            out_specs=blk(d),
        )
        acc = pl.pallas_call(
            functools.partial(_moe_kernel, rt, tmax, e, n_exp, e == n_exp - 1,
                              _col_chunks(dff, FF_COLS)),
            grid_spec=grid_spec,
            out_shape=jax.ShapeDtypeStruct((n, d), F32),
            compiler_params=_params(("parallel",)),
            name="moe",
        )(cnt, hn, rem, rtm, comb, acc, p["g_final"], p["w_e_gate"], p["w_e_up"], p["w_e_down"])
    return acc


def _rope_tables(pos):
    half = QK_ROPE // 2
    inv = ROPE_THETA ** (-jnp.arange(half, dtype=F32) / half)
    ang = pos.astype(F32)[:, None] * inv[None, :]
    cos, sin = jnp.cos(ang), jnp.sin(ang)
    z = jnp.zeros((pos.shape[0], ROPE_PAD - QK_ROPE), F32)
    return (jnp.concatenate([cos, cos, z], axis=-1), jnp.concatenate([-sin, sin, z], axis=-1))


def _swap_halves(w):
    half = w.shape[-1] // 2
    return jnp.concatenate([w[..., half:], w[..., :half]], axis=-1)


def _pad_rope(w):
    return jnp.pad(w, [(0, 0)] * (w.ndim - 1) + [(0, ROPE_PAD - QK_ROPE)])


def _prepare(g_mix_a, w_in_a, gamma_lb, g_onorm_a, w_out_a, g_kv_in, w_dkv, g_kv, w_ukv,
             g_mix_b, w_dq, g_q, w_uq, w_out_b, g_ffn, w_ff_gate, w_ff_up, w_ff_down,
             w_router, w_e_gate, w_e_up, w_e_down, g_final):
    kv_lora = g_kv.shape[-1]
    q_lora = g_q.shape[-1]
    row = lambda g: g.reshape(1, -1)
    wk_rope = w_dkv[:, kv_lora:]
    w_uq3 = w_uq[0].reshape(q_lora, MLA_HEADS, QK_NOPE + QK_ROPE)
    wq_rope = w_uq3[..., QK_NOPE:]
    w_ukv3 = w_ukv.reshape(kv_lora, MLA_HEADS, QK_NOPE + V_DIM)
    n_exp = w_router.shape[-1]
    return {
        "g_mix_a": g_mix_a[0], "w_in": w_in_a[0].astype(BF16), "gamma_lb": gamma_lb,
        "g_onorm": g_onorm_a[0], "w_out_a": w_out_a[0].astype(BF16),
        "g_ffn0": g_ffn[0], "w_ff_gate": w_ff_gate[0].astype(BF16),
        "w_ff_up": w_ff_up[0].astype(BF16), "w_ff_down": w_ff_down[0].astype(BF16),
        "g_kv_in": row(g_kv_in), "g_kv": row(g_kv), "g_mix_b": row(g_mix_b[0]), "g_q": row(g_q[0]),
        "w_kv": jnp.concatenate([w_dkv[:, :kv_lora], _pad_rope(wk_rope)], axis=-1).astype(BF16),
        "w_kvs": _pad_rope(_swap_halves(wk_rope)).astype(BF16),
        "w_dq": w_dq[0].astype(BF16),
        "w_qn": w_uq3[..., :QK_NOPE].reshape(q_lora, -1).astype(BF16),
        "w_qr": _pad_rope(wq_rope).reshape(q_lora, -1).astype(BF16),
        "w_qrs": _pad_rope(_swap_halves(wq_rope)).reshape(q_lora, -1).astype(BF16),
        "w_uk": jnp.transpose(w_ukv3[..., :QK_NOPE], (1, 2, 0)).astype(BF16),
        "w_uv": jnp.transpose(w_ukv3[..., QK_NOPE:], (1, 0, 2)).astype(BF16),
        "w_out_b": w_out_b[0].astype(BF16),
        "g_ffn1": row(g_ffn[1]), "g_final": row(g_final),
        "w_router": jnp.pad(w_router[0], ((0, 0), (0, LANE - n_exp))),
        "w_e_gate": w_e_gate[0].astype(BF16), "w_e_up": w_e_up[0].astype(BF16),
        "w_e_down": w_e_down[0].astype(BF16),
    }


def _trunk(x, pos, s0, p, attend, q_dtype, gla_bb, tm):
    B, T, d = x.shape
    n = B * T
    x2 = x.reshape(n, d)
    act_dtype = BF16 if T % HGRN_CHUNK == 0 else F32
    q, k, lf, v, sg = _hgrn_proj(x2, p["g_mix_a"], p["gamma_lb"], p["w_in"], 0, tm, act_dtype)
    r3 = lambda a: a.reshape(B, T, d)
    og, s_t = _gla(r3(q), r3(k), r3(lf), r3(v), r3(sg), p["g_onorm"], s0, gla_bb, 512)
    h2 = _ffn(x2, og.reshape(n, d), p["w_out_a"], p["g_ffn0"], p["w_ff_gate"], p["w_ff_up"],
              p["w_ff_down"], 2 * tm)
    cos_t, sin_t = _rope_tables(pos)
    c, kr, kcat, qcat = _kvq(h2, cos_t, sin_t, p, q_dtype, tm)
    o_lat = attend(qcat, kcat, c, kr)
    sb = min(1024, n)
    h3, hn, comb, rtm, rem, cnt = _route(h2, o_lat, p, sb)
    n_exp = p["w_e_gate"].shape[0]
    cnt_i = cnt[:, 0, :n_exp].astype(jnp.int32).reshape(-1)
    y = _moe(h3, hn, comb, rtm, rem, cnt_i, p, sb, 288)
    kv_lora = c.shape[-1]
    return (y.reshape(B, T, d), c.reshape(B, T, kv_lora), kr.reshape(B, T, QK_ROPE), s_t[None])


def kernel(x_prompt, x_sample, cache_ckv, cache_krope, state_hgrn, page_table, g_mix_a, w_in_a,
           gamma_lb, g_onorm_a, w_out_a, g_kv_in, w_dkv, g_kv, w_ukv, g_mix_b, w_dq, g_q, w_uq,
           w_out_b, g_ffn, w_ff_gate, w_ff_up, w_ff_down, w_router, w_e_gate, w_e_up, w_e_down,
           g_final):
    p = _prepare(g_mix_a, w_in_a, gamma_lb, g_onorm_a, w_out_a, g_kv_in, w_dkv, g_kv, w_ukv,
                 g_mix_b, w_dq, g_q, w_uq, w_out_b, g_ffn, w_ff_gate, w_ff_up, w_ff_down,
                 w_router, w_e_gate, w_e_up, w_e_down, g_final)
    kv_lora = g_kv.shape[-1]
    bp, tp, _ = x_prompt.shape
    bs, ts, _ = x_sample.shape

    def attend_p(qcat, kcat, c, kr):
        return _attn_prompt(qcat, kcat, bp, tp, kv_lora, 128, 512)

    y_p, c_p, kr_p, s_p = _trunk(x_prompt, jnp.arange(tp, dtype=F32), None, p, attend_p,
                                 BF16, 1, 512)

    past = page_table.shape[1] * PAGE_SIZE
    pos_s = jnp.tile(past + jnp.arange(ts, dtype=F32), bs)

    def attend_s(qcat, kcat, c, kr):
        return _attn_sample(qcat, c, kr, cache_ckv, cache_krope, page_table, 8, 4)

    y_s, c_s, kr_s, s_s = _trunk(x_sample, pos_s, state_hgrn[0], p, attend_s, F32, 8, 512)
    return (y_p, y_s, c_p, kr_p, c_s, kr_s, s_p.astype(state_hgrn.dtype),
            s_s.astype(state_hgrn.dtype))
```

```python
import functools

import numpy as np
import jax
import jax.numpy as jnp
from jax import lax
from jax.experimental import pallas as pl
from jax.experimental.pallas import tpu as pltpu

F32 = jnp.float32
BF16 = jnp.bfloat16

RMS_EPS = 1e-6
ROPE_THETA = 10000.0
HGRN_HEADS = 8
HGRN_DK = 128
HGRN_CHUNK = 128
MLA_HEADS = 8
QK_NOPE = 128
QK_ROPE = 64
V_DIM = 128
PAGE_SIZE = 128
TOP_K = 2
SM_SCALE = (QK_NOPE + QK_ROPE) ** -0.5
NEG_INF = -1e30
LANE = 128
SUBLANE = 8
ROPE_PAD = LANE
VMEM_LIMIT = 56 * 1024 * 1024
FF_CHUNK = 1408
FF_COLS = 512


def _params(sem):
    return pltpu.CompilerParams(dimension_semantics=sem, vmem_limit_bytes=VMEM_LIMIT)


def _dot(a, b):
    return jnp.dot(a, b, preferred_element_type=F32)


def _dot_nt(a, b):
    return lax.dot_general(a, b, (((1,), (1,)), ((), ())), preferred_element_type=F32)


def _dot_tn(a, b):
    return lax.dot_general(a, b, (((0,), (0,)), ((), ())), preferred_element_type=F32)


def _rms(x, g):
    return x * lax.rsqrt(jnp.mean(x * x, axis=-1, keepdims=True) + RMS_EPS) * g


def _sigmoid(x):
    return 1.0 / (1.0 + jnp.exp(-x))


def _lane_fold(x, op):
    w = x.shape[-1]
    if w % LANE or w == LANE:
        return x
    acc = x[:, :LANE]
    for j in range(1, w // LANE):
        acc = op(acc, x[:, j * LANE:(j + 1) * LANE])
    return acc


def _hgrn_proj_kernel(layer, x_ref, g_ref, gam_ref, w_ref, q_ref, k_ref, lf_ref, i_ref, sg_ref):
    d = x_ref.shape[-1]
    xb = _rms(x_ref[...], g_ref[...]).astype(BF16)

    def proj(j):
        return _dot(xb, w_ref[:, j * d:(j + 1) * d])

    gam = gam_ref[...]
    e = jnp.exp(gam - jnp.max(gam, axis=0, keepdims=True))
    lb = jnp.sum(e[:layer + 1], axis=0, keepdims=True) / jnp.sum(e, axis=0, keepdims=True)

    q = proj(0)
    q_ref[...] = (q * _sigmoid(q)).astype(q_ref.dtype)
    z = proj(1)
    t = jnp.exp(-jnp.abs(z))
    r = 1.0 / (1.0 + t)
    pos = z >= 0
    sig_z = jnp.where(pos, r, t * r)
    sig_mz = jnp.where(pos, t * r, r)
    k_ref[...] = ((1.0 - lb) * sig_mz).astype(k_ref.dtype)
    lf_ref[...] = jnp.log(lb + (1.0 - lb) * sig_z)
    i_ref[...] = proj(2).astype(i_ref.dtype)
    g = proj(3)
    sg_ref[...] = (g * _sigmoid(g)).astype(sg_ref.dtype)


def _hgrn_proj(x2, g_mix, gamma_lb, w_in_bf, layer, tm, act_dtype):
    n, d = x2.shape
    tm = min(tm, n)
    row = pl.BlockSpec((tm, d), lambda i: (i, 0))
    out = jax.ShapeDtypeStruct((n, d), act_dtype)
    return pl.pallas_call(
        functools.partial(_hgrn_proj_kernel, layer),
        grid=(n // tm,),
        in_specs=[row,
                  pl.BlockSpec((1, d), lambda i: (0, 0)),
                  pl.BlockSpec(gamma_lb.shape, lambda i: (0, 0)),
                  pl.BlockSpec(w_in_bf.shape, lambda i: (0, 0))],
        out_specs=[row] * 5,
        out_shape=[out, out, jax.ShapeDtypeStruct((n, d), F32), out, out],
        compiler_params=_params(("parallel",)),
        name="hgrn_proj",
    )(x2, g_mix.reshape(1, d), gamma_lb, w_in_bf)


def _gla_constants(L):
    nl = int(np.log2(L))
    assert 2 ** nl == L
    t = np.arange(L)
    me = np.zeros((1 + nl, L, L), np.float32)
    me[0] = (t[None, :] <= t[:, None])
    masks = np.zeros((1 + nl, L, L), np.float32)
    masks[0] = np.eye(L)
    for l in range(nl):
        m = L >> (l + 1)
        blk = t // (2 * m)
        r = blk * 2 * m + m
        upper = t >= r
        j = t[None, :]
        up_rows = (j >= r[:, None]) & (j <= t[:, None])
        lo_rows = (j > t[:, None]) & (j <= r[:, None] - 1)
        me[1 + l] = np.where(upper[:, None], up_rows, lo_rows)
        masks[1 + l] = (upper[:, None] & ~upper[None, :] & (blk[:, None] == blk[None, :]))
    return me.reshape((1 + nl) * L, L), masks, nl


def _gla_kernel(L, nl, nh, has_s0, mm, *refs):
    s_scrs = refs[-nh:]
    refs = refs[:-nh]
    if has_s0:
        (q_ref, k_ref, lf_ref, v_ref, sg_ref, go_ref, me_ref, mk_ref, s0_ref,
         og_ref, st_ref) = refs
    else:
        (q_ref, k_ref, lf_ref, v_ref, sg_ref, go_ref, me_ref, mk_ref,
         og_ref, st_ref) = refs
        s0_ref = None
    bb, tb, _ = q_ref.shape
    nchunk = tb // L
    ti = pl.program_id(1)
    dk = HGRN_DK

    @pl.when(ti == 0)
    def _():
        if has_s0:
            def init(b, carry):
                for h in range(nh):
                    s_scrs[h][b] = s0_ref[b, h].T
                return carry
            lax.fori_loop(0, bb, init, 0)
        else:
            for h in range(nh):
                s_scrs[h][...] = jnp.zeros_like(s_scrs[h])

    me = me_ref[...]
    go = go_ref[...]

    def chunk(b, c):
        sl = pl.ds(pl.multiple_of(c * L, L), L)
        g = lf_ref[b, sl, :]
        if mm == BF16:
            g1 = g.astype(BF16)
            g2 = (g - g1.astype(F32)).astype(BF16)
            ee = _dot(me, g1) + _dot(me, g2)
        else:
            ee = _dot(me, g)
        for h in range(nh):
            hs = slice(h * dk, (h + 1) * dk)
            q = q_ref[b, sl, hs].astype(F32)
            k = k_ref[b, sl, hs].astype(F32)
            v = v_ref[b, sl, hs].astype(mm)
            st = s_scrs[h][b]
            b_cum = ee[0:L, hs]
            b_last = b_cum[L - 1:L, :]
            o = _dot_nt((q * jnp.exp(b_cum)).astype(mm), st.astype(mm))
            sc = _dot_nt(q.astype(mm), k.astype(mm)) * mk_ref[0]
            for l in range(nl):
                e = jnp.exp(ee[(1 + l) * L:(2 + l) * L, hs])
                sc = sc + _dot_nt((q * e).astype(mm), (k * e).astype(mm)) * mk_ref[1 + l]
            o = o + _dot(sc.astype(mm), v)
            kd = (k * jnp.exp(b_last - b_cum)).astype(mm)
            s_scrs[h][b] = st * jnp.exp(b_last) + _dot_tn(v, kd)
            og = _rms(o, go) * sg_ref[b, sl, hs].astype(F32)
            og_ref[b, sl, hs] = og.astype(og_ref.dtype)

    def body(i, carry):
        chunk(i // nchunk, i % nchunk)
        return carry

    lax.fori_loop(0, bb * nchunk, body, 0)

    @pl.when(ti == pl.num_programs(1) - 1)
    def _():
        def fin(b, carry):
            for h in range(nh):
                st_ref[b, h] = s_scrs[h][b].T
            return carry
        lax.fori_loop(0, bb, fin, 0)


def _gla(q, k, lf, v, sg, g_onorm, s0, bb, tb):
    B, T, hd = q.shape
    H = hd // HGRN_DK
    L = HGRN_CHUNK if T % HGRN_CHUNK == 0 else T
    tb = min(tb, T)
    bb = min(bb, B)
    mm = BF16 if L % 16 == 0 else F32
    me_np, mk_np, nl = _gla_constants(L)
    me = jnp.asarray(me_np, mm)
    mk = jnp.asarray(mk_np, F32)
    seq = pl.BlockSpec((bb, tb, hd), lambda b, t: (b, t, 0))
    st_spec = pl.BlockSpec((bb, H, HGRN_DK, HGRN_DK), lambda b, t: (b, 0, 0, 0))
    in_specs = [seq] * 5 + [
        pl.BlockSpec((1, HGRN_DK), lambda b, t: (0, 0)),
        pl.BlockSpec(me.shape, lambda b, t: (0, 0)),
        pl.BlockSpec(mk.shape, lambda b, t: (0, 0, 0)),
    ]
    args = [q, k, lf, v, sg, g_onorm.reshape(1, HGRN_DK), me, mk]
    if s0 is not None:
        in_specs.append(st_spec)
        args.append(s0)
    return pl.pallas_call(
        functools.partial(_gla_kernel, L, nl, H, s0 is not None, mm),
        grid=(B // bb, T // tb),
        in_specs=in_specs,
        out_specs=[seq, st_spec],
        out_shape=[jax.ShapeDtypeStruct((B, T, hd), BF16),
                   jax.ShapeDtypeStruct((B, H, HGRN_DK, HGRN_DK), F32)],
        scratch_shapes=[pltpu.VMEM((bb, HGRN_DK, HGRN_DK), F32) for _ in range(H)],
        compiler_params=_params(("parallel", "arbitrary")),
        name="gla",
    )(*args)


def _col_chunks(width, target):
    step = max(LANE, target // LANE * LANE)
    return [(c, min(c + step, width)) for c in range(0, width, step)]


def _swiglu_chunks(x, wg_ref, wu_ref, wd_ref, chunks):
    y = None
    for c0, c1 in chunks:
        a = _dot(x, wg_ref[:, c0:c1])
        u = _dot(x, wu_ref[:, c0:c1])
        part = _dot((a * _sigmoid(a) * u).astype(BF16), wd_ref[c0:c1, :])
        y = part if y is None else y + part
    return y


def _ffn_kernel(chunks, h_ref, og_ref, wo_ref, g_ref, wg_ref, wu_ref, wd_ref, out_ref, hn_scr):
    f = pl.program_id(1)

    @pl.when(f == 0)
    def _():
        h1 = h_ref[...] + _dot(og_ref[...], wo_ref[...])
        out_ref[...] = h1
        hn_scr[...] = _rms(h1, g_ref[...]).astype(BF16)

    out_ref[...] += _swiglu_chunks(hn_scr[...], wg_ref, wu_ref, wd_ref, chunks)


def _pick_tf(dff, target):
    best = None
    for tf in range(LANE, dff + 1, LANE):
        if dff % tf == 0 and tf <= target:
            best = tf
    return best or dff


def _ffn(h, og, w_out_bf, g_ffn, wg_bf, wu_bf, wd_bf, tm):
    n, d = h.shape
    dff = wg_bf.shape[1]
    tm = min(tm, n)
    tf = _pick_tf(dff, FF_CHUNK)
    row = lambda i, f: (i, 0)
    return pl.pallas_call(
        functools.partial(_ffn_kernel, _col_chunks(tf, FF_COLS)),
        grid=(n // tm, dff // tf),
        in_specs=[pl.BlockSpec((tm, d), row),
                  pl.BlockSpec((tm, d), row),
                  pl.BlockSpec((d, d), lambda i, f: (0, 0), pipeline_mode=pl.Buffered(1)),
                  pl.BlockSpec((1, d), lambda i, f: (0, 0)),
                  pl.BlockSpec((d, tf), lambda i, f: (0, f)),
                  pl.BlockSpec((d, tf), lambda i, f: (0, f)),
                  pl.BlockSpec((tf, d), lambda i, f: (f, 0))],
        out_specs=pl.BlockSpec((tm, d), row),
        out_shape=jax.ShapeDtypeStruct((n, d), F32),
        scratch_shapes=[pltpu.VMEM((tm, d), BF16)],
        compiler_params=_params(("parallel", "arbitrary")),
        name="ffn",
    )(h, og, w_out_bf, g_ffn.reshape(1, d), wg_bf, wu_bf, wd_bf)


def _kvq_kernel(kv_lora, h_ref, cos_ref, sin_ref, gkin_ref, wkv_ref, wkvs_ref, gkv_ref,
                gmix_ref, wdq_ref, gq_ref, wqn_ref, wqr_ref, wqrs_ref, wuk_ref,
                c_ref, kr_ref, kcat_ref, qcat_ref):
    h = h_ref[...]
    cos = cos_ref[...]
    sin = sin_ref[...]
    nb = _rms(h, gkin_ref[...]).astype(BF16)
    kv = _dot(nb, wkv_ref[...])
    kvs = _dot(nb, wkvs_ref[...])
    c = _rms(kv[:, :kv_lora], gkv_ref[...])
    kr = kv[:, kv_lora:] * cos + kvs * sin
    c_ref[...] = c
    kr_ref[...] = kr[:, :QK_ROPE]
    kcat_ref[:, :kv_lora] = c.astype(BF16)
    kcat_ref[:, kv_lora:] = kr.astype(BF16)
    xb = _rms(h, gmix_ref[...]).astype(BF16)
    cq = _rms(_dot(xb, wdq_ref[...]), gq_ref[...]).astype(BF16)
    qn = _dot(cq, wqn_ref[...])
    qr = _dot(cq, wqr_ref[...])
    qrs = _dot(cq, wqrs_ref[...])
    for hh in range(MLA_HEADS):
        qn_h = qn[:, hh * QK_NOPE:(hh + 1) * QK_NOPE].astype(BF16)
        qa_h = _dot(qn_h, wuk_ref[hh]) * SM_SCALE
        sl = slice(hh * ROPE_PAD, (hh + 1) * ROPE_PAD)
        qr_h = (qr[:, sl] * cos + qrs[:, sl] * sin) * SM_SCALE
        qcat_ref[hh, :, :kv_lora] = qa_h.astype(qcat_ref.dtype)
        qcat_ref[hh, :, kv_lora:] = qr_h.astype(qcat_ref.dtype)


def _kvq(h2, cos_t, sin_t, p, q_dtype, tm):
    n, d = h2.shape
    tm = min(tm, cos_t.shape[0])
    kv_lora = p["g_kv"].shape[-1]
    kcw = kv_lora + ROPE_PAD
    nt = cos_t.shape[0] // tm
    const = lambda a: pl.BlockSpec(a.shape, lambda i: (0,) * a.ndim)
    row = lambda w: pl.BlockSpec((tm, w), lambda i: (i, 0))
    tab = pl.BlockSpec((tm, ROPE_PAD), lambda i: (i % nt, 0))
    weights = [p["g_kv_in"], p["w_kv"], p["w_kvs"], p["g_kv"], p["g_mix_b"], p["w_dq"],
               p["g_q"], p["w_qn"], p["w_qr"], p["w_qrs"], p["w_uk"]]
    return pl.pallas_call(
        functools.partial(_kvq_kernel, kv_lora),
        grid=(n // tm,),
        in_specs=[row(d), tab, tab] + [const(w) for w in weights],
        out_specs=[row(kv_lora), row(QK_ROPE), row(kcw),
                   pl.BlockSpec((MLA_HEADS, tm, kcw), lambda i: (0, i, 0))],
        out_shape=[jax.ShapeDtypeStruct((n, kv_lora), F32),
                   jax.ShapeDtypeStruct((n, QK_ROPE), F32),
                   jax.ShapeDtypeStruct((n, kcw), BF16),
                   jax.ShapeDtypeStruct((MLA_HEADS, n, kcw), q_dtype)],
        compiler_params=_params(("parallel",)),
        name="kvq",
    )(h2, cos_t, sin_t, *weights)


def _attn_prompt_kernel(tq, tk, kv_lora, q_ref, k_ref, o_ref, s0_scr, s1_scr, m_scr, l_scr,
                        acc_scr):
    i = pl.program_id(1)
    rows = MLA_HEADS * tq
    s_scrs = (s0_scr, s1_scr)
    n_full = (i * tq) // tk

    m_scr[...] = jnp.full_like(m_scr, NEG_INF)
    l_scr[...] = jnp.zeros_like(l_scr)
    acc_scr[...] = jnp.zeros_like(acc_scr)

    def k_tile(j):
        return k_ref[pl.ds(pl.multiple_of(j * tk, tk), tk), :]

    def stage_a(j, slot):
        q = q_ref[...].reshape(rows, q_ref.shape[-1])
        s_scrs[slot][...] = _dot_nt(q, k_tile(j))

    def stage_b(j, slot, masked):
        s = s_scrs[slot][...]
        if masked:
            q_pos = i * tq + lax.broadcasted_iota(jnp.int32, (rows, tk), 0) % tq
            k_pos = j * tk + lax.broadcasted_iota(jnp.int32, (rows, tk), 1)
            s = jnp.where(k_pos <= q_pos, s, NEG_INF)
        m_prev = m_scr[...]
        m_new = jnp.maximum(m_prev, jnp.max(s, axis=-1, keepdims=True))
        alpha = jnp.exp(m_prev - m_new)
        pr = jnp.exp(s - m_new)
        l_scr[...] = alpha * l_scr[...] + jnp.sum(pr, axis=-1, keepdims=True)
        acc_scr[...] = alpha * acc_scr[...] + _dot(pr.astype(BF16), k_tile(j)[:, :kv_lora])
        m_scr[...] = m_new

    stage_a(0, 0)

    def pair(t, carry):
        stage_a(2 * t + 1, 1)
        stage_b(2 * t, 0, False)
        stage_a(2 * t + 2, 0)
        stage_b(2 * t + 1, 1, False)
        return carry

    lax.fori_loop(0, n_full // 2, pair, 0)

    @pl.when(n_full % 2 == 1)
    def _():
        stage_a(n_full, 1)
        stage_b(n_full - 1, 0, False)
        stage_b(n_full, 1, True)

    @pl.when(n_full % 2 == 0)
    def _():
        stage_b(n_full, 0, True)

    o = acc_scr[...] / l_scr[...]
    o_ref[...] = o.reshape(MLA_HEADS, tq, kv_lora).astype(o_ref.dtype)


def _attn_prompt(qcat, kcat, B, T, kv_lora, tq, tk):
    tk = min(tk, T)
    tq = min(tq, tk)
    assert tk % tq == 0 and T % tk == 0
    nq = T // tq
    kcw = kcat.shape[-1]
    rows = MLA_HEADS * tq
    return pl.pallas_call(
        functools.partial(_attn_prompt_kernel, tq, tk, kv_lora),
        grid=(B, nq),
        in_specs=[pl.BlockSpec((MLA_HEADS, tq, kcw), lambda b, i: (0, b * nq + i, 0)),
                  pl.BlockSpec((T, kcw), lambda b, i: (b, 0))],
        out_specs=pl.BlockSpec((MLA_HEADS, tq, kv_lora), lambda b, i: (0, b * nq + i, 0)),
        out_shape=jax.ShapeDtypeStruct((MLA_HEADS, B * T, kv_lora), BF16),
        scratch_shapes=[pltpu.VMEM((rows, tk), F32), pltpu.VMEM((rows, tk), F32),
                        pltpu.VMEM((rows, 1), F32), pltpu.VMEM((rows, 1), F32),
                        pltpu.VMEM((rows, kv_lora), F32)],
        compiler_params=_params(("parallel", "arbitrary")),
        name="attn_prompt",
    )(qcat, kcat)


def _attn_sample_kernel(gp, gps, ns, kv_lora, pt_ref, q_ref, cn_ref, krn_ref, ckv_hbm, kr_hbm,
                        o_ref, *scr):
    craw, kraw, rest = scr[0:ns], scr[ns:2 * ns], scr[2 * ns:]
    cbf, kbf, s_scr = rest[0:2], rest[2:4], rest[4:6]
    sem_c, sem_k, m_scr, l_scr, acc_scr = rest[6:]
    b = pl.program_id(0)
    nb = pl.num_programs(0)
    total = nb * gps
    base = b * gps
    t_new = q_ref.shape[1]
    rows = MLA_HEADS * t_new

    def copies(grp, slot):
        out = []
        for j in range(gp):
            pid = pt_ref[grp * gp + j]
            out.append(pltpu.make_async_copy(ckv_hbm.at[pid], craw[slot].at[j], sem_c.at[slot]))
            out.append(pltpu.make_async_copy(kr_hbm.at[pid], kraw[slot].at[j], sem_k.at[slot]))
        return out

    def start(grp, slot):
        for c in copies(grp, slot):
            c.start()

    def wait(grp, slot):
        for c in copies(grp, slot):
            c.wait()

    def start_ahead(g):
        nxt = base + g + ns
        start(jnp.where(nxt < total, nxt, nxt - total), g % ns)

    @pl.when(b == 0)
    def _():
        for g in range(ns):
            start(g, g)

    q = q_ref[...].reshape(rows, q_ref.shape[-1])
    qa = q[:, :kv_lora]
    qr = q[:, kv_lora:kv_lora + QK_ROPE]
    cn = cn_ref[...]
    s_new = _dot_nt(qa, cn) + _dot_nt(qr, krn_ref[...])
    q_t = lax.broadcasted_iota(jnp.int32, s_new.shape, 0) % t_new
    k_t = lax.broadcasted_iota(jnp.int32, s_new.shape, 1)
    s_new = jnp.where(k_t <= q_t, s_new, NEG_INF)
    m0 = jnp.max(s_new, axis=-1, keepdims=True)
    p0 = jnp.exp(s_new - m0)
    m_scr[...] = m0
    l_scr[...] = jnp.sum(p0, axis=-1, keepdims=True)
    acc_scr[...] = _dot(p0, cn)
    qa_b = qa.astype(BF16)
    qr_b = qr.astype(BF16)

    def stage_a(raw, slot):
        for j in range(gp):
            cbf[slot][j * PAGE_SIZE:(j + 1) * PAGE_SIZE, :] = craw[raw][j].astype(BF16)
            kbf[slot][:, j * PAGE_SIZE:(j + 1) * PAGE_SIZE] = kraw[raw][j].astype(BF16)
        s_scr[slot][...] = _dot_nt(cbf[slot][...], qa_b).T + _dot(qr_b, kbf[slot][...])

    def stage_b(slot):
        s = s_scr[slot][...]
        m_prev = m_scr[...]
        m_new = jnp.maximum(m_prev, jnp.max(s, axis=-1, keepdims=True))
        alpha = jnp.exp(m_prev - m_new)
        pr = jnp.exp(s - m_new)
        l_scr[...] = alpha * l_scr[...] + jnp.sum(pr, axis=-1, keepdims=True)
        acc_scr[...] = alpha * acc_scr[...] + _dot(pr.astype(BF16), cbf[slot][...])
        m_scr[...] = m_new

    wait(base, 0)
    stage_a(0, 0)
    for g in range(gps - 1):
        start_ahead(g)
        wait(base + g + 1, (g + 1) % ns)
        stage_a((g + 1) % ns, (g + 1) % 2)
        stage_b(g % 2)
    start_ahead(gps - 1)
    stage_b((gps - 1) % 2)

    o = acc_scr[...] / l_scr[...]
    o_ref[...] = o.reshape(MLA_HEADS, t_new, kv_lora)

    @pl.when(b == nb - 1)
    def _():
        for g in range(ns):
            wait(g, g)


def _attn_sample(qcat, c_new, kr_new, cache_ckv, cache_krope, page_table, gp, ns):
    B, n_pages = page_table.shape
    t_new = c_new.shape[0] // B
    kv_lora = c_new.shape[-1]
    kcw = qcat.shape[-1]
    ns = min(ns, n_pages)
    ns -= ns % 2
    gp = min(gp, n_pages // ns)
    while n_pages % (ns * gp):
        gp -= 1
    gps = n_pages // gp
    rows = MLA_HEADS * t_new
    keys = gp * PAGE_SIZE
    pt_flat = page_table.reshape(-1)
    krope_t = jnp.swapaxes(cache_krope, 1, 2)
    many = lambda n, shape, dt: [pltpu.VMEM(shape, dt) for _ in range(n)]
    grid_spec = pltpu.PrefetchScalarGridSpec(
        num_scalar_prefetch=1,
        grid=(B,),
        in_specs=[pl.BlockSpec((MLA_HEADS, t_new, kcw), lambda b, pt: (0, b, 0)),
                  pl.BlockSpec((t_new, kv_lora), lambda b, pt: (b, 0)),
                  pl.BlockSpec((t_new, QK_ROPE), lambda b, pt: (b, 0)),
                  pl.BlockSpec(memory_space=pl.ANY),
                  pl.BlockSpec(memory_space=pl.ANY)],
        out_specs=pl.BlockSpec((MLA_HEADS, t_new, kv_lora), lambda b, pt: (0, b, 0)),
        scratch_shapes=many(ns, (gp, PAGE_SIZE, kv_lora), F32)
                       + many(ns, (gp, QK_ROPE, PAGE_SIZE), F32)
                       + many(2, (keys, kv_lora), BF16) + many(2, (QK_ROPE, keys), BF16)
                       + many(2, (rows, keys), F32)
                       + [pltpu.SemaphoreType.DMA((ns,)), pltpu.SemaphoreType.DMA((ns,)),
                          pltpu.VMEM((rows, 1), F32), pltpu.VMEM((rows, 1), F32),
                          pltpu.VMEM((rows, kv_lora), F32)],
    )
    return pl.pallas_call(
        functools.partial(_attn_sample_kernel, gp, gps, ns, kv_lora),
        grid_spec=grid_spec,
        out_shape=jax.ShapeDtypeStruct((MLA_HEADS, B * t_new, kv_lora), F32),
        compiler_params=_params(("arbitrary",)),
        name="attn_sample",
    )(pt_flat, qcat, c_new, kr_new, cache_ckv, krope_t)


def _route_kernel(n_exp, ne8, h_ref, o_ref, wuv_ref, wo_ref, g_ref, wr_ref, tri_ref,
                  h3_ref, hn_ref, comb_ref, rtm_ref, rem_ref, cnt_ref):
    heads = [_dot(o_ref[hh].astype(BF16), wuv_ref[hh]).astype(BF16) for hh in range(MLA_HEADS)]
    h3 = h_ref[...] + _dot(jnp.concatenate(heads, axis=-1), wo_ref[...])
    h3_ref[...] = h3
    hn = _rms(h3, g_ref[...])
    hn_ref[...] = hn.astype(BF16)
    lg = jnp.dot(hn, wr_ref[...], preferred_element_type=F32, precision=lax.Precision.HIGHEST)
    lane = lax.broadcasted_iota(jnp.int32, lg.shape, 1)
    lg = jnp.where(lane < n_exp, lg, -jnp.inf)
    m1 = jnp.max(lg, axis=-1, keepdims=True)
    i1 = jnp.min(jnp.where(lg == m1, lane, LANE), axis=-1, keepdims=True)
    lg2 = jnp.where(lane == i1, -jnp.inf, lg)
    m2 = jnp.max(lg2, axis=-1, keepdims=True)
    i2 = jnp.min(jnp.where(lg2 == m2, lane, LANE), axis=-1, keepdims=True)
    t = jnp.exp(m2 - m1)
    g1 = 1.0 / (1.0 + t)
    comb_ref[...] = jnp.where(lane == i1, g1, 0.0) + jnp.where(lane == i2, t * g1, 0.0)
    routed = jnp.where(lane == i1, 1.0, 0.0) + jnp.where(lane == i2, 1.0, 0.0)
    rank = _dot(tri_ref[...], routed.astype(BF16))
    rtm = jnp.where(routed > 0.0, rank, -1.0)
    rtm_ref[...] = rtm
    rem_ref[...] = rtm.T[:ne8]
    cnt_ref[...] = jnp.max(rtm, axis=0, keepdims=True) + 1.0


def _route(h2, o_lat, p, sb):
    n, d = h2.shape
    n_exp = p["w_e_gate"].shape[0]
    ne8 = -(-n_exp // SUBLANE) * SUBLANE
    kv_lora = o_lat.shape[-1]
    nb = n // sb
    tri = jnp.asarray(np.tril(np.ones((sb, sb), np.float32), -1), BF16)
    c2 = lambda i: (0, 0)
    row = lambda w: pl.BlockSpec((sb, w), lambda i: (i, 0))
    return pl.pallas_call(
        functools.partial(_route_kernel, n_exp, ne8),
        grid=(nb,),
        in_specs=[row(d),
                  pl.BlockSpec((MLA_HEADS, sb, kv_lora), lambda i: (0, i, 0)),
                  pl.BlockSpec(p["w_uv"].shape, lambda i: (0, 0, 0)),
                  pl.BlockSpec((d, d), c2),
                  pl.BlockSpec((1, d), c2),
                  pl.BlockSpec((d, LANE), c2),
                  pl.BlockSpec((sb, sb), c2)],
        out_specs=[row(d), row(d), row(LANE), row(LANE),
                   pl.BlockSpec((None, ne8, sb), lambda i: (i, 0, 0)),
                   pl.BlockSpec((None, 1, LANE), lambda i: (i, 0, 0))],
        out_shape=[jax.ShapeDtypeStruct((n, d), F32),
                   jax.ShapeDtypeStruct((n, d), BF16),
                   jax.ShapeDtypeStruct((n, LANE), F32),
                   jax.ShapeDtypeStruct((n, LANE), F32),
                   jax.ShapeDtypeStruct((nb, ne8, sb), F32),
                   jax.ShapeDtypeStruct((nb, 1, LANE), F32)],
        compiler_params=_params(("parallel",)),
        name="route",
    )(h2, o_lat, p["w_uv"], p["w_out_b"], p["g_ffn1"], p["w_router"], tri)


def _moe_kernel(rt, tmax, e, n_exp, last, chunks, cnt_ref, hn_ref, rem_ref, rtm_ref, comb_ref,
                acc_ref, gfin_ref, wg_ref, wu_ref, wd_ref, out_ref):
    j = pl.program_id(0)
    sb = hn_ref.shape[0]
    n_t = (cnt_ref[j * n_exp + e] + rt - 1) // rt
    out_ref[...] = acc_ref[...]
    rank_row = rem_ref[e:e + 1, :]
    sel = lax.broadcasted_iota(jnp.int32, (sb, LANE), 1) == e
    rank_col = jnp.sum(jnp.where(sel, rtm_ref[...], 0.0), axis=-1, keepdims=True)
    gate_col = jnp.sum(jnp.where(sel, comb_ref[...], 0.0), axis=-1, keepdims=True)
    for t in range(tmax):
        @pl.when(t < n_t)
        def _():
            r = (t * rt + lax.broadcasted_iota(jnp.int32, (rt, sb), 0)).astype(F32)
            onehot = jnp.where(rank_row == r, 1.0, 0.0).astype(BF16)
            x = _dot(onehot, hn_ref[...]).astype(BF16)
            y = _swiglu_chunks(x, wg_ref, wu_ref, wd_ref, chunks)
            rc = (t * rt + lax.broadcasted_iota(jnp.int32, (sb, rt), 1)).astype(F32)
            onehot_t = jnp.where(rank_col == rc, 1.0, 0.0).astype(BF16)
            out_ref[...] += gate_col * _dot(onehot_t, y.astype(BF16))
    if last:
        out_ref[...] = _rms(out_ref[...], gfin_ref[...])


def _moe(h3, hn, comb, rtm, rem, cnt, p, sb, rt):
    n, d = h3.shape
    n_exp, _, dff = p["w_e_gate"].shape
    rt = min(rt, sb)
    tmax = -(-sb // rt)
    ne8 = rem.shape[1]
    blk = lambda w: pl.BlockSpec((sb, w), lambda j, c: (j, 0))
    acc = h3
    for e in range(n_exp):
        expert = lambda j, c, e=e: (e, 0, 0)
        grid_spec = pltpu.PrefetchScalarGridSpec(
            num_scalar_prefetch=1,
            grid=(n // sb,),
            in_specs=[blk(d),
                      pl.BlockSpec((None, ne8, sb), lambda j, c: (j, 0, 0)),
                      blk(LANE), blk(LANE), blk(d),
                      pl.BlockSpec((1, d), lambda j, c: (0, 0)),
                      pl.BlockSpec((None, d, dff), expert, pipeline_mode=pl.Buffered(1)),
                      pl.BlockSpec((None, d, dff), expert, pipeline_mode=pl.Buffered(1)),
                      pl.BlockSpec((None, dff, d), expert, pipeline_mode=pl.Buffered(1))],
            out_specs=blk(d),
        )
        acc = pl.pallas_call(
            functools.partial(_moe_kernel, rt, tmax, e, n_exp, e == n_exp - 1,
                              _col_chunks(dff, FF_COLS)),
            grid_spec=grid_spec,
            out_shape=jax.ShapeDtypeStruct((n, d), F32),
            compiler_params=_params(("parallel",)),
            name="moe",
        )(cnt, hn, rem, rtm, comb, acc, p["g_final"], p["w_e_gate"], p["w_e_up"], p["w_e_down"])
    return acc


def _rope_tables(pos):
    half = QK_ROPE // 2
    inv = ROPE_THETA ** (-jnp.arange(half, dtype=F32) / half)
    ang = pos.astype(F32)[:, None] * inv[None, :]
    cos, sin = jnp.cos(ang), jnp.sin(ang)
    z = jnp.zeros((pos.shape[0], ROPE_PAD - QK_ROPE), F32)
    return (jnp.concatenate([cos, cos, z], axis=-1), jnp.concatenate([-sin, sin, z], axis=-1))


def _swap_halves(w):
    half = w.shape[-1] // 2
    return jnp.concatenate([w[..., half:], w[..., :half]], axis=-1)


def _pad_rope(w):
    return jnp.pad(w, [(0, 0)] * (w.ndim - 1) + [(0, ROPE_PAD - QK_ROPE)])


def _prepare(g_mix_a, w_in_a, gamma_lb, g_onorm_a, w_out_a, g_kv_in, w_dkv, g_kv, w_ukv,
             g_mix_b, w_dq, g_q, w_uq, w_out_b, g_ffn, w_ff_gate, w_ff_up, w_ff_down,
             w_router, w_e_gate, w_e_up, w_e_down, g_final):
    kv_lora = g_kv.shape[-1]
    q_lora = g_q.shape[-1]
    row = lambda g: g.reshape(1, -1)
    wk_rope = w_dkv[:, kv_lora:]
    w_uq3 = w_uq[0].reshape(q_lora, MLA_HEADS, QK_NOPE + QK_ROPE)
    wq_rope = w_uq3[..., QK_NOPE:]
    w_ukv3 = w_ukv.reshape(kv_lora, MLA_HEADS, QK_NOPE + V_DIM)
    n_exp = w_router.shape[-1]
    return {
        "g_mix_a": g_mix_a[0], "w_in": w_in_a[0].astype(BF16), "gamma_lb": gamma_lb,
        "g_onorm": g_onorm_a[0], "w_out_a": w_out_a[0].astype(BF16),
        "g_ffn0": g_ffn[0], "w_ff_gate": w_ff_gate[0].astype(BF16),
        "w_ff_up": w_ff_up[0].astype(BF16), "w_ff_down": w_ff_down[0].astype(BF16),
        "g_kv_in": row(g_kv_in), "g_kv": row(g_kv), "g_mix_b": row(g_mix_b[0]), "g_q": row(g_q[0]),
        "w_kv": jnp.concatenate([w_dkv[:, :kv_lora], _pad_rope(wk_rope)], axis=-1).astype(BF16),
        "w_kvs": _pad_rope(_swap_halves(wk_rope)).astype(BF16),
        "w_dq": w_dq[0].astype(BF16),
        "w_qn": w_uq3[..., :QK_NOPE].reshape(q_lora, -1).astype(BF16),
        "w_qr": _pad_rope(wq_rope).reshape(q_lora, -1).astype(BF16),
        "w_qrs": _pad_rope(_swap_halves(wq_rope)).reshape(q_lora, -1).astype(BF16),
        "w_uk": jnp.transpose(w_ukv3[..., :QK_NOPE], (1, 2, 0)).astype(BF16),
        "w_uv": jnp.transpose(w_ukv3[..., QK_NOPE:], (1, 0, 2)).astype(BF16),
        "w_out_b": w_out_b[0].astype(BF16),
        "g_ffn1": row(g_ffn[1]), "g_final": row(g_final),
        "w_router": jnp.pad(w_router[0], ((0, 0), (0, LANE - n_exp))),
        "w_e_gate": w_e_gate[0].astype(BF16), "w_e_up": w_e_up[0].astype(BF16),
        "w_e_down": w_e_down[0].astype(BF16),
    }


def _trunk(x, pos, s0, p, attend, q_dtype, gla_bb, tm):
    B, T, d = x.shape
    n = B * T
    x2 = x.reshape(n, d)
    act_dtype = BF16 if T % HGRN_CHUNK == 0 else F32
    q, k, lf, v, sg = _hgrn_proj(x2, p["g_mix_a"], p["gamma_lb"], p["w_in"], 0, tm, act_dtype)
    r3 = lambda a: a.reshape(B, T, d)
    og, s_t = _gla(r3(q), r3(k), r3(lf), r3(v), r3(sg), p["g_onorm"], s0, gla_bb, 512)
    h2 = _ffn(x2, og.reshape(n, d), p["w_out_a"], p["g_ffn0"], p["w_ff_gate"], p["w_ff_up"],
              p["w_ff_down"], 2 * tm)
    cos_t, sin_t = _rope_tables(pos)
    c, kr, kcat, qcat = _kvq(h2, cos_t, sin_t, p, q_dtype, tm)
    o_lat = attend(qcat, kcat, c, kr)
    sb = min(1024, n)
    h3, hn, comb, rtm, rem, cnt = _route(h2, o_lat, p, sb)
    n_exp = p["w_e_gate"].shape[0]
    cnt_i = cnt[:, 0, :n_exp].astype(jnp.int32).reshape(-1)
    y = _moe(h3, hn, comb, rtm, rem, cnt_i, p, sb, 288)
    kv_lora = c.shape[-1]
    return (y.reshape(B, T, d), c.reshape(B, T, kv_lora), kr.reshape(B, T, QK_ROPE), s_t[None])


def kernel(x_prompt, x_sample, cache_ckv, cache_krope, state_hgrn, page_table, g_mix_a, w_in_a,
           gamma_lb, g_onorm_a, w_out_a, g_kv_in, w_dkv, g_kv, w_ukv, g_mix_b, w_dq, g_q, w_uq,
           w_out_b, g_ffn, w_ff_gate, w_ff_up, w_ff_down, w_router, w_e_gate, w_e_up, w_e_down,
           g_final):
    p = _prepare(g_mix_a, w_in_a, gamma_lb, g_onorm_a, w_out_a, g_kv_in, w_dkv, g_kv, w_ukv,
                 g_mix_b, w_dq, g_q, w_uq, w_out_b, g_ffn, w_ff_gate, w_ff_up, w_ff_down,
                 w_router, w_e_gate, w_e_up, w_e_down, g_final)
    kv_lora = g_kv.shape[-1]
    bp, tp, _ = x_prompt.shape
    bs, ts, _ = x_sample.shape

    def attend_p(qcat, kcat, c, kr):
        return _attn_prompt(qcat, kcat, bp, tp, kv_lora, 128, 512)

    y_p, c_p, kr_p, s_p = _trunk(x_prompt, jnp.arange(tp, dtype=F32), None, p, attend_p,
                                 BF16, 1, 512)

    past = page_table.shape[1] * PAGE_SIZE
    pos_s = jnp.tile(past + jnp.arange(ts, dtype=F32), bs)

    def attend_s(qcat, kcat, c, kr):
        return _attn_sample(qcat, c, kr, cache_ckv, cache_krope, page_table, 8, 4)

    y_s, c_s, kr_s, s_s = _trunk(x_sample, pos_s, state_hgrn[0], p, attend_s, F32, 8, 512)
    return (y_p, y_s, c_p, kr_p, c_s, kr_s, s_p.astype(state_hgrn.dtype),
            s_s.astype(state_hgrn.dtype))
```
